```python
import jax, jax.numpy as jnp
from jax import lax
import numpy as np

D_MODEL = 4096
BATCH = 2
SEQ = 8192
DEPTH = 1
DEC_BATCH = 2
DEC_SEQ = 4096
PAST_LEN = 128

HEAD_DIM = 128
N_HEADS_A = 16
N_HEADS_B = 16
WIDTH_A = N_HEADS_A * HEAD_DIM
WIDTH_B = N_HEADS_B * HEAD_DIM
MIX_WIDTH = WIDTH_A + WIDTH_B
D_FF = 11008
DILATED_CONFIGS = ((128, 1), (512, 4), (2048, 16))
BLK = 128
ROPE_THETA = 500000.0
ROT_DIM = HEAD_DIM // 4
GRID_W = 64
WIN_ROWS = 8
WIN_COLS = 16
NORM_EPS = 1e-6
NEG_INF = -1e30

kernel_name = "hybrid_dilated_neighbourhood_encoder"


def _rmsnorm(x, g):
    xf = x.astype(jnp.float32)
    y = xf * lax.rsqrt(jnp.mean(xf * xf, axis=-1, keepdims=True) + NORM_EPS)
    return (y * g.astype(jnp.float32)).astype(x.dtype)


def _swiglu(x, w_gate, w_up, w_down):
    return (jax.nn.silu(x @ w_gate) * (x @ w_up)) @ w_down


def _rope_partial(x):
    T = x.shape[1]
    pos = jnp.arange(T, dtype=jnp.float32)
    inv = ROPE_THETA ** (-jnp.arange(0, ROT_DIM, 2, dtype=jnp.float32) / ROT_DIM)
    ang = pos[:, None] * inv[None, :]
    cos = jnp.cos(ang)[None, :, None, :]
    sin = jnp.sin(ang)[None, :, None, :]
    xf = x.astype(jnp.float32)
    x1 = xf[..., :ROT_DIM // 2]
    x2 = xf[..., ROT_DIM // 2:ROT_DIM]
    out = jnp.concatenate([x1 * cos - x2 * sin, x2 * cos + x1 * sin, xf[..., ROT_DIM:]], axis=-1)
    return out.astype(x.dtype)


def _window_partials(q, k, v, half):
    B, G, L, H, dh = q.shape
    Lp = ((L + BLK - 1) // BLK) * BLK
    nb = Lp // BLK
    qb = jnp.pad(q, ((0, 0), (0, 0), (0, Lp - L), (0, 0), (0, 0))).reshape(B, G, nb, BLK, H, dh)
    pad = ((0, 0), (0, 0), (BLK, Lp - L + BLK), (0, 0), (0, 0))
    kp = jnp.pad(k, pad).reshape(B, G, nb + 2, BLK, H, dh)
    vp = jnp.pad(v, pad).reshape(B, G, nb + 2, BLK, H, dh)
    kn = jnp.concatenate([kp[:, :, :-2], kp[:, :, 1:-1], kp[:, :, 2:]], axis=3)
    vn = jnp.concatenate([vp[:, :, :-2], vp[:, :, 1:-1], vp[:, :, 2:]], axis=3)
    blk = jnp.arange(nb)[:, None] * BLK
    qpos = blk + jnp.arange(BLK)[None, :]
    kpos = blk - BLK + jnp.arange(3 * BLK)[None, :]
    valid_k = (kpos >= 0) & (kpos < L)
    mask = (jnp.abs(qpos[:, :, None] - kpos[:, None, :]) <= half) & valid_k[:, None, :]
    s = jnp.einsum('bgnqhd,bgnkhd->bgnhqk', qb, kn).astype(jnp.float32) * (HEAD_DIM ** -0.5)
    s = jnp.where(mask[None, None, :, None], s, NEG_INF)
    m = jnp.max(s, axis=-1)
    p = jnp.exp(s - m[..., None])
    den = jnp.sum(p, axis=-1)
    o = jnp.einsum('bgnhqk,bgnkhd->bgnqhd', p.astype(v.dtype), vn).astype(jnp.float32)
    o = o / jnp.swapaxes(den, -1, -2)[..., None]
    o = o.reshape(B, G, Lp, H, dh)[:, :, :L]
    m = jnp.swapaxes(m, -1, -2).reshape(B, G, Lp, H)[:, :, :L]
    den = jnp.swapaxes(den, -1, -2).reshape(B, G, Lp, H)[:, :, :L]
    return o, m, den


def _dilated_attention(q, k, v):
    B, T, H, dh = q.shape
    outs, ms, dens = [], [], []
    for window, dil in DILATED_CONFIGS:
        half = window // (2 * dil)
        L = T // dil

        def strided(x):
            return x.reshape(B, L, dil, H, dh).transpose(0, 2, 1, 3, 4)

        o, m, den = _window_partials(strided(q), strided(k), strided(v), half)
        outs.append(o.transpose(0, 2, 1, 3, 4).reshape(B, T, H, dh))
        ms.append(m.transpose(0, 2, 1, 3).reshape(B, T, H))
        dens.append(den.transpose(0, 2, 1, 3).reshape(B, T, H))
    m_all = jnp.stack(ms)
    den_all = jnp.stack(dens)
    o_all = jnp.stack(outs)
    w = den_all * jnp.exp(m_all - jnp.max(m_all, axis=0, keepdims=True))
    o = jnp.sum(w[..., None] * o_all, axis=0) / jnp.sum(w, axis=0)[..., None]
    return o.astype(q.dtype)


def _neighbourhood_attention(q, k, v, rel_bias):
    B, T, H, dh = q.shape
    rows = T // GRID_W
    wr = min(WIN_ROWS, rows)
    r = jnp.arange(rows)
    rs = jnp.clip(r - wr // 2, 0, rows - wr)
    row_idx = rs[:, None] + jnp.arange(wr)[None, :]
    c = jnp.arange(GRID_W)
    cs = jnp.clip(c - WIN_COLS // 2, 0, GRID_W - WIN_COLS)
    col_mask = (c[None, :] >= cs[:, None]) & (c[None, :] < cs[:, None] + WIN_COLS)
    qg = q.reshape(B, rows, GRID_W, H, dh)
    kr = k.reshape(B, rows, GRID_W, H, dh)[:, row_idx]
    vr = v.reshape(B, rows, GRID_W, H, dh)[:, row_idx]
    s = jnp.einsum('brqhd,brjkhd->brhqjk', qg, kr).astype(jnp.float32) * (HEAD_DIM ** -0.5)
    dr = row_idx - r[:, None] + (WIN_ROWS - 1)
    dc = jnp.clip(c[None, :] - c[:, None], -(WIN_COLS - 1), WIN_COLS - 1) + (WIN_COLS - 1)
    b = rel_bias.astype(jnp.float32)[:, dr]
    b = b[:, :, :, dc].transpose(1, 0, 3, 2, 4)
    s = s + b[None]
    s = jnp.where(col_mask[None, None, None, :, None, :], s, NEG_INF)
    p = jax.nn.softmax(s.reshape(B, rows, H, GRID_W, wr * GRID_W), axis=-1)
    p = p.reshape(B, rows, H, GRID_W, wr, GRID_W).astype(v.dtype)
    o = jnp.einsum('brhqjk,brjkhd->brqhd', p, vr)
    return o.reshape(B, T, H, dh)


def _layer(h, ffn1_norm, ffn1_w_gate, ffn1_w_up, ffn1_w_down, mix_norm, w_in, nbr_rel_bias,
           out_norm_a, out_norm_b, w_out, ffn2_norm, ffn2_w_gate, ffn2_w_up, ffn2_w_down):
    B, T, _ = h.shape
    h = h + 0.5 * _swiglu(_rmsnorm(h, ffn1_norm), ffn1_w_gate, ffn1_w_up, ffn1_w_down)
    u = _rmsnorm(h, mix_norm)
    qkv = u @ w_in
    splits = [WIDTH_A, 2 * WIDTH_A, 3 * WIDTH_A, 3 * WIDTH_A + WIDTH_B, 3 * WIDTH_A + 2 * WIDTH_B]
    qa, ka, va, qb, kb, vb = jnp.split(qkv, splits, axis=-1)
    heads_a = lambda t: t.reshape(B, T, N_HEADS_A, HEAD_DIM)
    heads_b = lambda t: t.reshape(B, T, N_HEADS_B, HEAD_DIM)
    qa, ka, va = _rope_partial(heads_a(qa)), _rope_partial(heads_a(ka)), heads_a(va)
    oa = _dilated_attention(qa, ka, va).reshape(B, T, WIDTH_A)
    ob = _neighbourhood_attention(heads_b(qb), heads_b(kb), heads_b(vb), nbr_rel_bias).reshape(B, T, WIDTH_B)
    merged = jnp.concatenate([_rmsnorm(oa, out_norm_a), _rmsnorm(ob, out_norm_b)], axis=-1)
    h = h + merged @ w_out
    h = h + 0.5 * _swiglu(_rmsnorm(h, ffn2_norm), ffn2_w_gate, ffn2_w_up, ffn2_w_down)
    return h


def setup_inputs(seed: int = 0) -> dict:
    key = jax.random.key(seed)
    ks = jax.random.split(key, 20)
    f32 = jnp.float32

    def normal(k, shape, scale):
        return jax.random.normal(k, shape, f32) * scale

    def gain(k, width):
        return 1.0 + 0.01 * jax.random.normal(k, (DEPTH, width), f32)

    return {
        "x_prompt": jax.random.normal(ks[0], (BATCH, SEQ, D_MODEL), f32),
        "x_sample": jax.random.normal(ks[1], (DEC_BATCH, DEC_SEQ, D_MODEL), f32),
        "ffn1_norm": gain(ks[2], D_MODEL),
        "ffn1_w_gate": normal(ks[3], (DEPTH, D_MODEL, D_FF), D_MODEL ** -0.5),
        "ffn1_w_up": normal(ks[4], (DEPTH, D_MODEL, D_FF), D_MODEL ** -0.5),
        "ffn1_w_down": normal(ks[5], (DEPTH, D_FF, D_MODEL), D_FF ** -0.5),
        "mix_norm": gain(ks[6], D_MODEL),
        "w_in": normal(ks[7], (DEPTH, D_MODEL, 3 * MIX_WIDTH), D_MODEL ** -0.5),
        "nbr_rel_bias": normal(ks[8], (DEPTH, N_HEADS_B, 2 * WIN_ROWS - 1, 2 * WIN_COLS - 1), 0.1),
        "out_norm_a": gain(ks[9], WIDTH_A),
        "out_norm_b": gain(ks[10], WIDTH_B),
        "w_out": normal(ks[11], (DEPTH, MIX_WIDTH, D_MODEL), MIX_WIDTH ** -0.5),
        "ffn2_norm": gain(ks[12], D_MODEL),
        "ffn2_w_gate": normal(ks[13], (DEPTH, D_MODEL, D_FF), D_MODEL ** -0.5),
        "ffn2_w_up": normal(ks[14], (DEPTH, D_MODEL, D_FF), D_MODEL ** -0.5),
        "ffn2_w_down": normal(ks[15], (DEPTH, D_FF, D_MODEL), D_FF ** -0.5),
        "final_norm": 1.0 + 0.01 * jax.random.normal(ks[16], (D_MODEL,), f32),
    }


def reference(x_prompt, x_sample, ffn1_norm, ffn1_w_gate, ffn1_w_up, ffn1_w_down, mix_norm, w_in,
              nbr_rel_bias, out_norm_a, out_norm_b, w_out, ffn2_norm, ffn2_w_gate, ffn2_w_up,
              ffn2_w_down, final_norm):
    def run(x):
        h = x
        for layer in range(DEPTH):
            h = _layer(h, ffn1_norm[layer], ffn1_w_gate[layer], ffn1_w_up[layer], ffn1_w_down[layer],
                       mix_norm[layer], w_in[layer], nbr_rel_bias[layer], out_norm_a[layer],
                       out_norm_b[layer], w_out[layer], ffn2_norm[layer], ffn2_w_gate[layer],
                       ffn2_w_up[layer], ffn2_w_down[layer])
        return _rmsnorm(h, final_norm)

    y_prompt = run(x_prompt)
    y_sample = run(x_sample)
    return (y_prompt, y_sample)
```

```python
import functools

import jax
import jax.numpy as jnp
from jax import lax
from jax.experimental import pallas as pl
from jax.experimental.pallas import tpu as pltpu

HEAD_DIM = 128
DILATED_CONFIGS = ((128, 1), (512, 4), (2048, 16))
QBLK = 128
ROPE_THETA = 500000.0
ROT_DIM = HEAD_DIM // 4
GRID_W = 64
WIN_ROWS = 8
WIN_COLS = 16
NORM_EPS = 1e-6
NEG_INF = -1e30
LANES = 128
V7X_VMEM_BYTES = 64 * 1024 * 1024

F32 = jnp.float32
BF16 = jnp.bfloat16


def _params(sem, vmem_mb):
    return pltpu.CompilerParams(dimension_semantics=sem,
                                vmem_limit_bytes=min(vmem_mb * 1024 * 1024, V7X_VMEM_BYTES - (2 << 20)))


def _pick(n, prefs):
    for p in prefs:
        if n % p == 0:
            return p
    return n


def _rms(x, g):
    ms = jnp.mean(x * x, axis=-1, keepdims=True)
    return x * lax.rsqrt(ms + NORM_EPS) * g


def _rmsnorm_kernel(x_ref, g_ref, o_ref):
    o_ref[...] = _rms(x_ref[...], g_ref[...]).astype(o_ref.dtype)


def _rmsnorm(x, g, out_dtype):
    n, d = x.shape
    bm = _pick(n, (256, 128, 8))
    return pl.pallas_call(
        _rmsnorm_kernel,
        grid=(n // bm,),
        in_specs=[pl.BlockSpec((bm, d), lambda i: (i, 0)),
                  pl.BlockSpec((1, d), lambda i: (0, 0))],
        out_specs=pl.BlockSpec((bm, d), lambda i: (i, 0)),
        out_shape=jax.ShapeDtypeStruct((n, d), out_dtype),
        compiler_params=_params(("parallel",), 40),
        name="rmsnorm",
    )(x, g.reshape(1, d))


def _add_norm_kernel(x_ref, f_ref, g_ref, h_ref, u_ref):
    h = x_ref[...] + 0.5 * f_ref[...]
    h_ref[...] = h
    u_ref[...] = _rms(h, g_ref[...]).astype(u_ref.dtype)


def _add_norm(x, f, g, out_dtype):
    n, d = x.shape
    bm = _pick(n, (256, 128, 8))
    row = pl.BlockSpec((bm, d), lambda i: (i, 0))
    return pl.pallas_call(
        _add_norm_kernel,
        grid=(n // bm,),
        in_specs=[row, row, pl.BlockSpec((1, d), lambda i: (0, 0))],
        out_specs=[row, row],
        out_shape=[jax.ShapeDtypeStruct((n, d), F32), jax.ShapeDtypeStruct((n, d), out_dtype)],
        compiler_params=_params(("parallel",), 48),
        name="add_norm",
    )(x, f, g.reshape(1, d))


def _ffn_kernel(xn_ref, wg_ref, wu_ref, wd_ref, o_ref):
    j = pl.program_id(1)

    @pl.when(j == 0)
    def _():
        o_ref[...] = jnp.zeros_like(o_ref)

    xn = xn_ref[...]
    g = jnp.dot(xn, wg_ref[...], preferred_element_type=F32)
    u = jnp.dot(xn, wu_ref[...], preferred_element_type=F32)
    a = (g * jax.nn.sigmoid(g) * u).astype(BF16)
    o_ref[...] += jnp.dot(a, wd_ref[...], preferred_element_type=F32)


def _ffn(xn, wg, wu, wd, bf):
    n, d = xn.shape
    fp = wg.shape[1]
    bm = _pick(n, (512, 256, 128, 8))
    return pl.pallas_call(
        _ffn_kernel,
        grid=(n // bm, fp // bf),
        in_specs=[pl.BlockSpec((bm, d), lambda i, j: (i, 0)),
                  pl.BlockSpec((d, bf), lambda i, j: (0, j)),
                  pl.BlockSpec((d, bf), lambda i, j: (0, j)),
                  pl.BlockSpec((bf, d), lambda i, j: (j, 0))],
        out_specs=pl.BlockSpec((bm, d), lambda i, j: (i, 0)),
        out_shape=jax.ShapeDtypeStruct((n, d), F32),
        compiler_params=_params(("parallel", "arbitrary"), 60),
        name="swiglu_ffn",
    )(xn, wg, wu, wd)


def _qkv_kernel(a_ref, w_ref, c_ref, sa_ref, sb_ref, o_ref, *, rope_tiles, heads_per_tile):
    j = pl.program_id(1)
    acc = jnp.dot(a_ref[...], w_ref[...], preferred_element_type=F32)

    @pl.when(j < rope_tiles)
    def _():
        c, sa, sb = c_ref[...], sa_ref[...], sb_ref[...]
        for h in range(heads_per_tile):
            x = acc[:, h * HEAD_DIM:(h + 1) * HEAD_DIM]
            y = (x * c + pltpu.roll(x, HEAD_DIM - ROT_DIM // 2, 1) * sa
                 + pltpu.roll(x, ROT_DIM // 2, 1) * sb)
            o_ref[h] = y.astype(o_ref.dtype)

    @pl.when(j >= rope_tiles)
    def _():
        for h in range(heads_per_tile):
            o_ref[h] = acc[:, h * HEAD_DIM:(h + 1) * HEAD_DIM].astype(o_ref.dtype)


def _qkv_proj(u, w, rope, t, n_rope_cols):
    n, d = u.shape
    m = w.shape[1]
    bm = _pick(t, (1024, 512, 256, 128))
    bn = _pick(n_rope_cols, (1024, 512, 256, 128))
    assert m % bn == 0 and n % bm == 0
    tb = t // bm
    tab = pl.BlockSpec((bm, HEAD_DIM), lambda i, j: (i % tb, 0))
    kern = functools.partial(_qkv_kernel, rope_tiles=n_rope_cols // bn, heads_per_tile=bn // HEAD_DIM)
    return pl.pallas_call(
        kern,
        grid=(n // bm, m // bn),
        in_specs=[pl.BlockSpec((bm, d), lambda i, j: (i, 0)),
                  pl.BlockSpec((d, bn), lambda i, j: (0, j)),
                  tab, tab, tab],
        out_specs=pl.BlockSpec((bn // HEAD_DIM, bm, HEAD_DIM), lambda i, j: (j, i, 0)),
        out_shape=jax.ShapeDtypeStruct((m // HEAD_DIM, n, HEAD_DIM), BF16),
        compiler_params=_params(("parallel", "arbitrary"), 56),
        name="qkv_proj",
    )(u, w, *rope)


def _out_proj_kernel(a_ref, w_ref, r_ref, o_ref):
    o_ref[...] = r_ref[...] + jnp.dot(a_ref[...], w_ref[...], preferred_element_type=F32)


def _out_proj(a, w, res):
    n, k = a.shape
    m = w.shape[1]
    bm = _pick(n, (1024, 512, 256, 128, 8))
    bn = _pick(m, (1024, 512, 256, 128))
    return pl.pallas_call(
        _out_proj_kernel,
        grid=(n // bm, m // bn),
        in_specs=[pl.BlockSpec((bm, k), lambda i, j: (i, 0)),
                  pl.BlockSpec((k, bn), lambda i, j: (0, j)),
                  pl.BlockSpec((bm, bn), lambda i, j: (i, j))],
        out_specs=pl.BlockSpec((bm, bn), lambda i, j: (i, j)),
        out_shape=jax.ShapeDtypeStruct((n, m), F32),
        compiler_params=_params(("parallel", "arbitrary"), 56),
        name="out_proj",
    )(a, w, res)


def _rope_tables(t):
    pos = jnp.arange(t, dtype=F32)
    inv = ROPE_THETA ** (-jnp.arange(0, ROT_DIM, 2, dtype=F32) / ROT_DIM)
    ang = pos[:, None] * inv[None, :]
    cos, sin = jnp.cos(ang), jnp.sin(ang)
    half = ROT_DIM // 2
    z_half = jnp.zeros((t, half), F32)
    z_rest = jnp.zeros((t, HEAD_DIM - ROT_DIM), F32)
    c = jnp.concatenate([cos, cos, jnp.ones((t, HEAD_DIM - ROT_DIM), F32)], axis=1)
    sa = jnp.concatenate([-sin, z_half, z_rest], axis=1)
    sb = jnp.concatenate([z_half, sin, z_rest], axis=1)
    return c, sa, sb


def _dilated_kernel(q_ref, k_ref, v_ref, o_ref, lse_ref, *, half, nblk):
    h = pl.program_id(2)
    l = nblk * QBLK
    win = 2 * QBLK
    scale = HEAD_DIM ** -0.5

    @pl.when(h == 0)
    def _():
        lse_ref[...] = jnp.zeros_like(lse_ref)

    lane = lax.broadcasted_iota(jnp.int32, (QBLK, LANES), 1)
    qi = lax.broadcasted_iota(jnp.int32, (QBLK, win), 0)
    ki = lax.broadcasted_iota(jnp.int32, (QBLK, win), 1)

    def body(i, carry):
        q0 = pl.multiple_of(i * QBLK, QBLK)
        k0 = pl.multiple_of(jnp.clip(q0 - half, 0, l - win), half)
        q = q_ref[pl.ds(q0, QBLK), :]
        k = k_ref[pl.ds(k0, win), :]
        v = v_ref[pl.ds(k0, win), :]
        s = lax.dot_general(q, k, (((1,), (1,)), ((), ())), preferred_element_type=F32) * scale
        s = jnp.where(jnp.abs(qi - ki + (q0 - k0)) <= half, s, NEG_INF)
        m = jnp.max(s, axis=-1, keepdims=True)
        p = jnp.exp(s - m)
        den = jnp.sum(p, axis=-1, keepdims=True)
        o = jnp.dot(p.astype(BF16), v, preferred_element_type=F32)
        o_ref[pl.ds(q0, QBLK), :] = o / den
        lse = m + jnp.log(den)
        rows = pl.ds(q0, QBLK)
        lse_ref[rows, :] = jnp.where(lane == h, lse, lse_ref[rows, :])
        return carry

    lax.fori_loop(0, nblk, body, 0)


def _dilated_attention(qkvh, b, t, n_heads, window, dil):
    n = b * t
    l = t // dil
    half = window // (2 * dil)
    assert l % QBLK == 0 and l >= 2 * QBLK and half <= QBLK // 2 and QBLK % half == 0
    nh = n_heads
    qv = qkvh.reshape(qkvh.shape[0], n // dil, dil * HEAD_DIM)
    blk = (None, l, HEAD_DIM)
    kern = functools.partial(_dilated_kernel, half=half, nblk=l // QBLK)
    o, lse = pl.pallas_call(
        kern,
        grid=(b, dil, nh),
        in_specs=[pl.BlockSpec(blk, lambda bi, r, h: (h, bi, r)),
                  pl.BlockSpec(blk, lambda bi, r, h: (nh + h, bi, r)),
                  pl.BlockSpec(blk, lambda bi, r, h: (2 * nh + h, bi, r))],
        out_specs=[pl.BlockSpec((l, HEAD_DIM), lambda bi, r, h: (bi, r * nh + h)),
                   pl.BlockSpec((l, LANES), lambda bi, r, h: (bi, r))],
        out_shape=[jax.ShapeDtypeStruct((n // dil, dil * nh * HEAD_DIM), F32),
                   jax.ShapeDtypeStruct((n // dil, dil * LANES), F32)],
        compiler_params=_params(("parallel", "parallel", "arbitrary"), 48),
        name=f"dilated_attn_d{dil}",
    )(qv, qv, qv)
    return o.reshape(n, nh * HEAD_DIM), lse.reshape(n, LANES)


def _nbr_kernel(q_ref, k_ref, v_ref, bias_ref, o_ref, *, rows):
    scale = HEAD_DIM ** -0.5
    wk = WIN_ROWS * GRID_W

    def body(r, carry):
        rs = jnp.clip(r - WIN_ROWS // 2, 0, rows - WIN_ROWS)
        q0 = pl.multiple_of(r * GRID_W, GRID_W)
        k0 = pl.multiple_of(rs * GRID_W, GRID_W)
        q = q_ref[pl.ds(q0, GRID_W), :]
        k = k_ref[pl.ds(k0, wk), :]
        v = v_ref[pl.ds(k0, wk), :]
        s = lax.dot_general(q, k, (((1,), (1,)), ((), ())), preferred_element_type=F32) * scale
        s = s + bias_ref[rs - r + (WIN_ROWS - 1)]
        m = jnp.max(s, axis=-1, keepdims=True)
        p = jnp.exp(s - m)
        den = jnp.sum(p, axis=-1, keepdims=True)
        o = jnp.dot(p.astype(BF16), v, preferred_element_type=F32)
        o_ref[pl.ds(q0, GRID_W), :] = o / den
        return carry

    lax.fori_loop(0, rows, body, 0)


def _nbr_bias_tables(rel_bias):
    nh = rel_bias.shape[0]
    c = jnp.arange(GRID_W)
    cs = jnp.clip(c - WIN_COLS // 2, 0, GRID_W - WIN_COLS)
    col_mask = (c[None, :] >= cs[:, None]) & (c[None, :] < cs[:, None] + WIN_COLS)
    dc = jnp.clip(c[None, :] - c[:, None], -(WIN_COLS - 1), WIN_COLS - 1) + (WIN_COLS - 1)
    tab = rel_bias.astype(F32)[:, :, dc]
    tab = jnp.where(col_mask[None, None], tab, NEG_INF)
    idx = jnp.arange(WIN_ROWS)[:, None] + jnp.arange(WIN_ROWS)[None, :]
    tab = tab[:, idx]
    return tab.transpose(0, 1, 3, 2, 4).reshape(nh, WIN_ROWS, GRID_W, WIN_ROWS * GRID_W)


def _nbr_attention(qkvh, bias_tab, b, t, head0, n_heads):
    n = b * t
    rows = t // GRID_W
    assert t % GRID_W == 0 and rows >= WIN_ROWS
    nh = n_heads
    blk = (None, t, HEAD_DIM)
    kern = functools.partial(_nbr_kernel, rows=rows)
    return pl.pallas_call(
        kern,
        grid=(b, nh),
        in_specs=[pl.BlockSpec(blk, lambda bi, h: (head0 + h, bi, 0)),
                  pl.BlockSpec(blk, lambda bi, h: (head0 + nh + h, bi, 0)),
                  pl.BlockSpec(blk, lambda bi, h: (head0 + 2 * nh + h, bi, 0)),
                  pl.BlockSpec((None, WIN_ROWS, GRID_W, WIN_ROWS * GRID_W), lambda bi, h: (h, 0, 0, 0))],
        out_specs=pl.BlockSpec((t, HEAD_DIM), lambda bi, h: (bi, h)),
        out_shape=jax.ShapeDtypeStruct((n, nh * HEAD_DIM), F32),
        compiler_params=_params(("parallel", "arbitrary"), 48),
        name="nbr_attn",
    )(qkvh, qkvh, qkvh, bias_tab)


def _merge_kernel(o1_ref, o2_ref, o3_ref, l1_ref, l2_ref, l3_ref, ob_ref, ga_ref, gb_ref, out_ref, *, n_heads):
    wa = n_heads * HEAD_DIM
    l1, l2, l3 = l1_ref[...], l2_ref[...], l3_ref[...]
    mx = jnp.maximum(jnp.maximum(l1, l2), l3)
    w1, w2, w3 = jnp.exp(l1 - mx), jnp.exp(l2 - mx), jnp.exp(l3 - mx)
    tot = w1 + w2 + w3
    parts = []
    for h in range(n_heads):
        cols = slice(h * HEAD_DIM, (h + 1) * HEAD_DIM)
        num = (w1[:, h:h + 1] * o1_ref[:, cols] + w2[:, h:h + 1] * o2_ref[:, cols]
               + w3[:, h:h + 1] * o3_ref[:, cols])
        parts.append(num / tot[:, h:h + 1])
    oa = jnp.concatenate(parts, axis=1)
    out_ref[:, :wa] = _rms(oa, ga_ref[...]).astype(out_ref.dtype)
    out_ref[:, wa:] = _rms(ob_ref[...], gb_ref[...]).astype(out_ref.dtype)


def _merge(parts, ob, ga, gb, n_heads):
    (o1, l1), (o2, l2), (o3, l3) = parts
    n, wa = o1.shape
    wb = ob.shape[1]
    bm = _pick(n, (256, 128, 8))
    ro = pl.BlockSpec((bm, wa), lambda i: (i, 0))
    rl = pl.BlockSpec((bm, LANES), lambda i: (i, 0))
    return pl.pallas_call(
        functools.partial(_merge_kernel, n_heads=n_heads),
        grid=(n // bm,),
        in_specs=[ro, ro, ro, rl, rl, rl,
                  pl.BlockSpec((bm, wb), lambda i: (i, 0)),
                  pl.BlockSpec((1, wa), lambda i: (0, 0)),
                  pl.BlockSpec((1, wb), lambda i: (0, 0))],
        out_specs=pl.BlockSpec((bm, wa + wb), lambda i: (i, 0)),
        out_shape=jax.ShapeDtypeStruct((n, wa + wb), BF16),
        compiler_params=_params(("parallel",), 48),
        name="merge_norm",
    )(o1, o2, o3, l1, l2, l3, ob, ga.reshape(1, wa), gb.reshape(1, wb))


FFN_CHUNK = 512


def _prep_ffn(w_gate, w_up, w_down):
    f = w_gate.shape[1]
    fp = -(-f // FFN_CHUNK) * FFN_CHUNK
    pad = fp - f
    return (jnp.pad(w_gate.astype(BF16), ((0, 0), (0, pad))),
            jnp.pad(w_up.astype(BF16), ((0, 0), (0, pad))),
            jnp.pad(w_down.astype(BF16), ((0, pad), (0, 0))))


def _layer(x, b, t, wts, next_gain, last):
    n, d = x.shape
    (ffn1_norm, ffn1, mix_norm, w_in, bias_tab, out_norm_a, out_norm_b, w_out, ffn2_norm, ffn2) = wts
    wa = out_norm_a.shape[0]
    wb = out_norm_b.shape[0]
    nha, nhb = wa // HEAD_DIM, wb // HEAD_DIM

    xn = _rmsnorm(x, ffn1_norm, BF16)
    f1 = _ffn(xn, *ffn1, FFN_CHUNK)
    h1, u = _add_norm(x, f1, mix_norm, BF16)

    qkvh = _qkv_proj(u, w_in, _rope_tables(t), t, 2 * wa)
    parts = [_dilated_attention(qkvh, b, t, nha, window, dil) for window, dil in DILATED_CONFIGS]
    ob = _nbr_attention(qkvh, bias_tab, b, t, 3 * nha, nhb)
    merged = _merge(parts, ob, out_norm_a, out_norm_b, nha)
    h2 = _out_proj(merged, w_out, h1)

    xn2 = _rmsnorm(h2, ffn2_norm, BF16)
    f2 = _ffn(xn2, *ffn2, FFN_CHUNK)
    return _add_norm(h2, f2, next_gain, F32 if last else BF16)


def kernel(x_prompt, x_sample, ffn1_norm, ffn1_w_gate, ffn1_w_up, ffn1_w_down, mix_norm, w_in, nbr_rel_bias, out_norm_a, out_norm_b, w_out, ffn2_norm, ffn2_w_gate, ffn2_w_up, ffn2_w_down, final_norm):
    depth = ffn1_norm.shape[0]
    layers = []
    for i in range(depth):
        layers.append((ffn1_norm[i],
                       _prep_ffn(ffn1_w_gate[i], ffn1_w_up[i], ffn1_w_down[i]),
                       mix_norm[i], w_in[i].astype(BF16), _nbr_bias_tables(nbr_rel_bias[i]),
                       out_norm_a[i], out_norm_b[i], w_out[i].astype(BF16), ffn2_norm[i],
                       _prep_ffn(ffn2_w_gate[i], ffn2_w_up[i], ffn2_w_down[i])))

    def run(x3):
        b, t, d = x3.shape
        h = x3.reshape(b * t, d)
        for i in range(depth):
            last = i == depth - 1
            h, y = _layer(h, b, t, layers[i], final_norm if last else ffn1_norm[i + 1], last)
        return y.reshape(b, t, d)

    return run(x_prompt), run(x_sample)
```

```python
import functools

import jax
import jax.numpy as jnp
from jax import lax
from jax.experimental import pallas as pl
from jax.experimental.pallas import tpu as pltpu

HEAD_DIM = 128
DILATED_CONFIGS = ((128, 1), (512, 4), (2048, 16))
QBLK = 128
ROPE_THETA = 500000.0
ROT_DIM = HEAD_DIM // 4
GRID_W = 64
WIN_ROWS = 8
WIN_COLS = 16
NORM_EPS = 1e-6
NEG_INF = -1e30
LANES = 128
V7X_VMEM_BYTES = 64 * 1024 * 1024

F32 = jnp.float32
BF16 = jnp.bfloat16


def _params(sem, vmem_mb):
    return pltpu.CompilerParams(dimension_semantics=sem,
                                vmem_limit_bytes=min(vmem_mb * 1024 * 1024, V7X_VMEM_BYTES - (2 << 20)))


def _pick(n, prefs):
    for p in prefs:
        if n % p == 0:
            return p
    return n


def _rms(x, g):
    ms = jnp.mean(x * x, axis=-1, keepdims=True)
    return x * lax.rsqrt(ms + NORM_EPS) * g


def _rmsnorm_kernel(x_ref, g_ref, o_ref):
    o_ref[...] = _rms(x_ref[...], g_ref[...]).astype(o_ref.dtype)


def _rmsnorm(x, g, out_dtype):
    n, d = x.shape
    bm = _pick(n, (256, 128, 8))
    return pl.pallas_call(
        _rmsnorm_kernel,
        grid=(n // bm,),
        in_specs=[pl.BlockSpec((bm, d), lambda i: (i, 0)),
                  pl.BlockSpec((1, d), lambda i: (0, 0))],
        out_specs=pl.BlockSpec((bm, d), lambda i: (i, 0)),
        out_shape=jax.ShapeDtypeStruct((n, d), out_dtype),
        compiler_params=_params(("parallel",), 40),
        name="rmsnorm",
    )(x, g.reshape(1, d))


def _add_norm_kernel(x_ref, f_ref, g_ref, h_ref, u_ref):
    h = x_ref[...] + 0.5 * f_ref[...]
    h_ref[...] = h
    u_ref[...] = _rms(h, g_ref[...]).astype(u_ref.dtype)


def _add_norm_last_kernel(x_ref, f_ref, g_ref, u_ref):
    u_ref[...] = _rms(x_ref[...] + 0.5 * f_ref[...], g_ref[...]).astype(u_ref.dtype)


def _add_norm(x, f, g, out_dtype, emit_h=True):
    n, d = x.shape
    bm = _pick(n, (256, 128, 8))
    row = pl.BlockSpec((bm, d), lambda i: (i, 0))
    normed = jax.ShapeDtypeStruct((n, d), out_dtype)
    return pl.pallas_call(
        _add_norm_kernel if emit_h else _add_norm_last_kernel,
        grid=(n // bm,),
        in_specs=[row, row, pl.BlockSpec((1, d), lambda i: (0, 0))],
        out_specs=[row, row] if emit_h else row,
        out_shape=[jax.ShapeDtypeStruct((n, d), F32), normed] if emit_h else normed,
        compiler_params=_params(("parallel",), 48),
        name="add_norm",
    )(x, f, g.reshape(1, d))


def _ffn_kernel(xn_ref, wg_ref, wu_ref, wd_ref, o_ref):
    j = pl.program_id(1)

    @pl.when(j == 0)
    def _():
        o_ref[...] = jnp.zeros_like(o_ref)

    xn = xn_ref[...]
    g = jnp.dot(xn, wg_ref[...], preferred_element_type=F32)
    u = jnp.dot(xn, wu_ref[...], preferred_element_type=F32)
    a = (g * jax.nn.sigmoid(g) * u).astype(BF16)
    o_ref[...] += jnp.dot(a, wd_ref[...], preferred_element_type=F32)


def _ffn(xn, wg, wu, wd, bf):
    n, d = xn.shape
    fp = wg.shape[1]
    bm = _pick(n, (512, 256, 128, 8))
    return pl.pallas_call(
        _ffn_kernel,
        grid=(n // bm, fp // bf),
        in_specs=[pl.BlockSpec((bm, d), lambda i, j: (i, 0)),
                  pl.BlockSpec((d, bf), lambda i, j: (0, j)),
                  pl.BlockSpec((d, bf), lambda i, j: (0, j)),
                  pl.BlockSpec((bf, d), lambda i, j: (j, 0))],
        out_specs=pl.BlockSpec((bm, d), lambda i, j: (i, 0)),
        out_shape=jax.ShapeDtypeStruct((n, d), F32),
        compiler_params=_params(("parallel", "arbitrary"), 60),
        name="swiglu_ffn",
    )(xn, wg, wu, wd)


def _qkv_kernel(a_ref, w_ref, c_ref, sa_ref, sb_ref, o_ref, *, rope_tiles, heads_per_tile):
    j = pl.program_id(1)
    acc = jnp.dot(a_ref[...], w_ref[...], preferred_element_type=F32)

    @pl.when(j < rope_tiles)
    def _():
        c, sa, sb = c_ref[...], sa_ref[...], sb_ref[...]
        for h in range(heads_per_tile):
            x = acc[:, h * HEAD_DIM:(h + 1) * HEAD_DIM]
            y = (x * c + pltpu.roll(x, HEAD_DIM - ROT_DIM // 2, 1) * sa
                 + pltpu.roll(x, ROT_DIM // 2, 1) * sb)
            o_ref[h] = y.astype(o_ref.dtype)

    @pl.when(j >= rope_tiles)
    def _():
        for h in range(heads_per_tile):
            o_ref[h] = acc[:, h * HEAD_DIM:(h + 1) * HEAD_DIM].astype(o_ref.dtype)


def _qkv_proj(u, w, rope, t, n_rope_cols, out_dtype):
    n, d = u.shape
    m = w.shape[1]
    bm = _pick(t, (1024, 512, 256, 128))
    bn = next(p for p in (1024, 512, 256, 128) if m % p == 0 and n_rope_cols % p == 0)
    assert n % bm == 0
    tb = t // bm
    tab = pl.BlockSpec((bm, HEAD_DIM), lambda i, j: (i % tb, 0))
    kern = functools.partial(_qkv_kernel, rope_tiles=n_rope_cols // bn, heads_per_tile=bn // HEAD_DIM)
    return pl.pallas_call(
        kern,
        grid=(n // bm, m // bn),
        in_specs=[pl.BlockSpec((bm, d), lambda i, j: (i, 0)),
                  pl.BlockSpec((d, bn), lambda i, j: (0, j)),
                  tab, tab, tab],
        out_specs=pl.BlockSpec((bn // HEAD_DIM, bm, HEAD_DIM), lambda i, j: (j, i, 0)),
        out_shape=jax.ShapeDtypeStruct((m // HEAD_DIM, n, HEAD_DIM), out_dtype),
        compiler_params=_params(("parallel", "arbitrary"), 56),
        name="qkv_proj",
    )(u, w, *rope)


def _out_proj_kernel(a_ref, w_ref, r_ref, o_ref):
    o_ref[...] = r_ref[...] + jnp.dot(a_ref[...], w_ref[...], preferred_element_type=F32)


def _out_proj(a, w, res):
    n, k = a.shape
    m = w.shape[1]
    bm = _pick(n, (1024, 512, 256, 128, 8))
    bn = _pick(m, (1024, 512, 256, 128))
    return pl.pallas_call(
        _out_proj_kernel,
        grid=(n // bm, m // bn),
        in_specs=[pl.BlockSpec((bm, k), lambda i, j: (i, 0)),
                  pl.BlockSpec((k, bn), lambda i, j: (0, j)),
                  pl.BlockSpec((bm, bn), lambda i, j: (i, j))],
        out_specs=pl.BlockSpec((bm, bn), lambda i, j: (i, j)),
        out_shape=jax.ShapeDtypeStruct((n, m), F32),
        compiler_params=_params(("parallel", "arbitrary"), 56),
        name="out_proj",
    )(a, w, res)


def _rope_tables(t):
    pos = jnp.arange(t, dtype=F32)
    inv = ROPE_THETA ** (-jnp.arange(0, ROT_DIM, 2, dtype=F32) / ROT_DIM)
    ang = pos[:, None] * inv[None, :]
    cos, sin = jnp.cos(ang), jnp.sin(ang)
    half = ROT_DIM // 2
    z_half = jnp.zeros((t, half), F32)
    z_rest = jnp.zeros((t, HEAD_DIM - ROT_DIM), F32)
    c = jnp.concatenate([cos, cos, jnp.ones((t, HEAD_DIM - ROT_DIM), F32)], axis=1)
    sa = jnp.concatenate([-sin, z_half, z_rest], axis=1)
    sb = jnp.concatenate([z_half, sin, z_rest], axis=1)
    return c, sa, sb


DIL_UNROLL = 4


def _dilated_kernel(q_ref, k_ref, v_ref, o_ref, m_scr, l_scr, *, t, configs):
    scale = HEAD_DIM ** -0.5
    win = 2 * QBLK
    qi = lax.broadcasted_iota(jnp.int32, (QBLK, win), 0)
    ki = lax.broadcasted_iota(jnp.int32, (QBLK, win), 1)

    for ci, (half, d) in enumerate(configs):
        l = t // d
        nblk = l // QBLK
        first, final = ci == 0, ci == len(configs) - 1
        shift = d.bit_length() - 1

        def rows(r, start, size, d=d):
            if d == 1:
                return pl.ds(pl.multiple_of(start, 8), size)
            return pl.ds(r + d * start, size, stride=d)

        def scores(idx, half=half, d=d, l=l, shift=shift, rows=rows):
            i = lax.shift_right_logical(idx, shift)
            r = idx - (i << shift)
            q0 = i * QBLK
            k0 = jnp.clip(q0 - half, 0, l - win)
            rq, rk = rows(r, q0, QBLK), rows(r, k0, win)
            q = q_ref[rq, :].astype(BF16)
            k = k_ref[rk, :].astype(BF16)
            s = lax.dot_general(q, k, (((1,), (1,)), ((), ())), preferred_element_type=F32) * scale
            return rq, rk, q0 - k0, s

        def softmax(rq, rk, off, s, half=half):
            s = jnp.where(jnp.abs(qi - ki + off) <= half, s, NEG_INF)
            m = jnp.max(s, axis=-1, keepdims=True)
            p = jnp.exp(s - m)
            den = jnp.sum(p, axis=-1, keepdims=True)
            return rq, rk, m, den, p.astype(BF16)

        def values(rq, rk, m, den, p):
            v = v_ref[rk, :].astype(BF16)
            return rq, m, den, jnp.dot(p, v, preferred_element_type=F32)

        def merge(rq, m, den, acc, first=first, final=final):
            if first:
                o_ref[rq, :] = acc
                m_scr[rq, :] = jnp.broadcast_to(m, (QBLK, LANES))
                l_scr[rq, :] = jnp.broadcast_to(den, (QBLK, LANES))
                return
            m_run = m_scr[rq, :]
            m_new = jnp.maximum(m_run, m)
            a_run = jnp.exp(m_run - m_new)
            a_blk = jnp.exp(m - m_new)
            num = o_ref[rq, :] * a_run + acc * a_blk
            tot = l_scr[rq, :] * a_run + den * a_blk
            if final:
                o_ref[rq, :] = num / tot
            else:
                o_ref[rq, :] = num
                m_scr[rq, :] = m_new
                l_scr[rq, :] = tot

        total = d * nblk
        unroll = DIL_UNROLL if total % DIL_UNROLL == 0 else 1

        def trip(it, carry, scores=scores, softmax=softmax, values=values, merge=merge, unroll=unroll):
            parts = [scores(it * unroll + u) for u in range(unroll)]
            parts = [softmax(*part) for part in parts]
            parts = [values(*part) for part in parts]
            for part in parts:
                merge(*part)
            return carry

        lax.fori_loop(0, total // unroll, trip, 0)


def _dilated_attention(qkv_a, b, t, n_heads):
    n = b * t
    nh = n_heads
    configs = tuple((window // (2 * dil), dil) for window, dil in DILATED_CONFIGS)
    for half, d in configs:
        assert d & (d - 1) == 0 and t % (d * QBLK) == 0 and t // d >= 2 * QBLK and half <= QBLK // 2
    blk = (None, t, HEAD_DIM)
    return pl.pallas_call(
        functools.partial(_dilated_kernel, t=t, configs=configs),
        grid=(b, nh),
        in_specs=[pl.BlockSpec(blk, lambda bi, h: (h, bi, 0)),
                  pl.BlockSpec(blk, lambda bi, h: (nh + h, bi, 0)),
                  pl.BlockSpec(blk, lambda bi, h: (2 * nh + h, bi, 0))],
        out_specs=pl.BlockSpec((t, HEAD_DIM), lambda bi, h: (bi, h)),
        out_shape=jax.ShapeDtypeStruct((n, nh * HEAD_DIM), F32),
        scratch_shapes=[pltpu.VMEM((t, LANES), F32), pltpu.VMEM((t, LANES), F32)],
        compiler_params=_params(("parallel", "arbitrary"), 56),
        name="dilated_attn",
    )(qkv_a, qkv_a, qkv_a)


NBR_UNROLL = 4


def _nbr_kernel(q_ref, k_ref, v_ref, bias_ref, o_ref, *, rows):
    scale = HEAD_DIM ** -0.5
    wk = WIN_ROWS * GRID_W

    def scores(r):
        rs = jnp.clip(r - WIN_ROWS // 2, 0, rows - WIN_ROWS)
        q0 = pl.multiple_of(r * GRID_W, GRID_W)
        k0 = pl.multiple_of(rs * GRID_W, GRID_W)
        q = q_ref[pl.ds(q0, GRID_W), :]
        k = k_ref[pl.ds(k0, wk), :]
        s = lax.dot_general(q, k, (((1,), (1,)), ((), ())), preferred_element_type=F32) * scale
        return q0, k0, s + bias_ref[rs - r + (WIN_ROWS - 1)]

    def softmax(q0, k0, s):
        m = jnp.max(s, axis=-1, keepdims=True)
        p = jnp.exp(s - m)
        return q0, k0, jnp.sum(p, axis=-1, keepdims=True), p.astype(BF16)

    def values(q0, k0, den, p):
        o = jnp.dot(p, v_ref[pl.ds(k0, wk), :], preferred_element_type=F32)
        o_ref[pl.ds(q0, GRID_W), :] = o / den

    unroll = NBR_UNROLL if rows % NBR_UNROLL == 0 else 1

    def trip(it, carry):
        parts = [scores(it * unroll + u) for u in range(unroll)]
        parts = [softmax(*part) for part in parts]
        for part in parts:
            values(*part)
        return carry

    lax.fori_loop(0, rows // unroll, trip, 0)


def _nbr_bias_tables(rel_bias):
    nh = rel_bias.shape[0]
    c = jnp.arange(GRID_W)
    cs = jnp.clip(c - WIN_COLS // 2, 0, GRID_W - WIN_COLS)
    col_mask = (c[None, :] >= cs[:, None]) & (c[None, :] < cs[:, None] + WIN_COLS)
    dc = jnp.clip(c[None, :] - c[:, None], -(WIN_COLS - 1), WIN_COLS - 1) + (WIN_COLS - 1)
    tab = rel_bias.astype(F32)[:, :, dc]
    tab = jnp.where(col_mask[None, None], tab, NEG_INF)
    idx = jnp.arange(WIN_ROWS)[:, None] + jnp.arange(WIN_ROWS)[None, :]
    tab = tab[:, idx]
    return tab.transpose(0, 1, 3, 2, 4).reshape(nh, WIN_ROWS, GRID_W, WIN_ROWS * GRID_W)


def _nbr_attention(qkvh, bias_tab, b, t, head0, n_heads):
    n = b * t
    rows = t // GRID_W
    assert t % GRID_W == 0 and rows >= WIN_ROWS
    nh = n_heads
    blk = (None, t, HEAD_DIM)
    kern = functools.partial(_nbr_kernel, rows=rows)
    return pl.pallas_call(
        kern,
        grid=(b, nh),
        in_specs=[pl.BlockSpec(blk, lambda bi, h: (head0 + h, bi, 0)),
                  pl.BlockSpec(blk, lambda bi, h: (head0 + nh + h, bi, 0)),
                  pl.BlockSpec(blk, lambda bi, h: (head0 + 2 * nh + h, bi, 0)),
                  pl.BlockSpec((None, WIN_ROWS, GRID_W, WIN_ROWS * GRID_W), lambda bi, h: (h, 0, 0, 0))],
        out_specs=pl.BlockSpec((t, HEAD_DIM), lambda bi, h: (bi, h)),
        out_shape=jax.ShapeDtypeStruct((n, nh * HEAD_DIM), F32),
        compiler_params=_params(("parallel", "arbitrary"), 48),
        name="nbr_attn",
    )(qkvh, qkvh, qkvh, bias_tab)


def _out_norms_kernel(oa_ref, ob_ref, ga_ref, gb_ref, out_ref):
    wa = oa_ref.shape[1]
    out_ref[:, :wa] = _rms(oa_ref[...], ga_ref[...]).astype(out_ref.dtype)
    out_ref[:, wa:] = _rms(ob_ref[...], gb_ref[...]).astype(out_ref.dtype)


def _out_norms(oa, ob, ga, gb):
    n, wa = oa.shape
    wb = ob.shape[1]
    bm = _pick(n, (256, 128, 8))
    return pl.pallas_call(
        _out_norms_kernel,
        grid=(n // bm,),
        in_specs=[pl.BlockSpec((bm, wa), lambda i: (i, 0)),
                  pl.BlockSpec((bm, wb), lambda i: (i, 0)),
                  pl.BlockSpec((1, wa), lambda i: (0, 0)),
                  pl.BlockSpec((1, wb), lambda i: (0, 0))],
        out_specs=pl.BlockSpec((bm, wa + wb), lambda i: (i, 0)),
        out_shape=jax.ShapeDtypeStruct((n, wa + wb), BF16),
        compiler_params=_params(("parallel",), 40),
        name="out_norms",
    )(oa, ob, ga.reshape(1, wa), gb.reshape(1, wb))


FFN_CHUNK = 512


def _prep_ffn(w_gate, w_up, w_down):
    f = w_gate.shape[1]
    fp = -(-f // FFN_CHUNK) * FFN_CHUNK
    pad = fp - f
    zc = jnp.zeros((w_gate.shape[0], pad), BF16)
    zr = jnp.zeros((pad, w_down.shape[1]), BF16)
    return (jnp.concatenate([w_gate.astype(BF16), zc], axis=1),
            jnp.concatenate([w_up.astype(BF16), zc], axis=1),
            jnp.concatenate([w_down.astype(BF16), zr], axis=0))


def _layer(x, b, t, wts, next_gain, last):
    (ffn1_norm, ffn1, mix_norm, w_in_a, w_in_b, bias_tab, out_norm_a, out_norm_b, w_out, ffn2_norm,
     ffn2) = wts
    wa = out_norm_a.shape[0]
    wb = out_norm_b.shape[0]
    nha, nhb = wa // HEAD_DIM, wb // HEAD_DIM
    rope = _rope_tables(t)

    xn = _rmsnorm(x, ffn1_norm, BF16)
    f1 = _ffn(xn, *ffn1, FFN_CHUNK)
    h1, u = _add_norm(x, f1, mix_norm, BF16)

    qkv_a = _qkv_proj(u, w_in_a, rope, t, 2 * wa, F32)
    qkv_b = _qkv_proj(u, w_in_b, rope, t, 0, BF16)
    oa = _dilated_attention(qkv_a, b, t, nha)
    ob = _nbr_attention(qkv_b, bias_tab, b, t, 0, nhb)
    merged = _out_norms(oa, ob, out_norm_a, out_norm_b)
    h2 = _out_proj(merged, w_out, h1)

    xn2 = _rmsnorm(h2, ffn2_norm, BF16)
    f2 = _ffn(xn2, *ffn2, FFN_CHUNK)
    if last:
        return None, _add_norm(h2, f2, next_gain, F32, emit_h=False)
    return _add_norm(h2, f2, next_gain, BF16)


def kernel(x_prompt, x_sample, ffn1_norm, ffn1_w_gate, ffn1_w_up, ffn1_w_down, mix_norm, w_in, nbr_rel_bias, out_norm_a, out_norm_b, w_out, ffn2_norm, ffn2_w_gate, ffn2_w_up, ffn2_w_down, final_norm):
    depth = ffn1_norm.shape[0]
    layers = []
    for i in range(depth):
        cols_a = 3 * out_norm_a.shape[1]
        layers.append((ffn1_norm[i],
                       _prep_ffn(ffn1_w_gate[i], ffn1_w_up[i], ffn1_w_down[i]),
                       mix_norm[i], w_in[i, :, :cols_a].astype(BF16), w_in[i, :, cols_a:].astype(BF16),
                       _nbr_bias_tables(nbr_rel_bias[i]),
                       out_norm_a[i], out_norm_b[i], w_out[i].astype(BF16), ffn2_norm[i],
                       _prep_ffn(ffn2_w_gate[i], ffn2_w_up[i], ffn2_w_down[i])))

    def run(x3):
        b, t, d = x3.shape
        h = x3.reshape(b * t, d)
        for i in range(depth):
            last = i == depth - 1
            h, y = _layer(h, b, t, layers[i], final_norm if last else ffn1_norm[i + 1], last)
        return y.reshape(b, t, d)

    return run(x_prompt), run(x_sample)
```

```python
import functools

import jax
import jax.numpy as jnp
from jax import lax
from jax.experimental import pallas as pl
from jax.experimental.pallas import tpu as pltpu

HEAD_DIM = 128
DILATED_CONFIGS = ((128, 1), (512, 4), (2048, 16))
QBLK = 128
ROPE_THETA = 500000.0
ROT_DIM = HEAD_DIM // 4
GRID_W = 64
WIN_ROWS = 8
WIN_COLS = 16
NORM_EPS = 1e-6
NEG_INF = -1e30
LANES = 128
V7X_VMEM_BYTES = 64 * 1024 * 1024

F32 = jnp.float32
BF16 = jnp.bfloat16


def _params(sem, vmem_mb):
    return pltpu.CompilerParams(dimension_semantics=sem,
                                vmem_limit_bytes=min(vmem_mb * 1024 * 1024, V7X_VMEM_BYTES - (2 << 20)))


def _pick(n, prefs):
    for p in prefs:
        if n % p == 0:
            return p
    return n


def _rms(x, g):
    ms = jnp.mean(x * x, axis=-1, keepdims=True)
    return x * lax.rsqrt(ms + NORM_EPS) * g


def _rmsnorm_kernel(x_ref, g_ref, o_ref):
    o_ref[...] = _rms(x_ref[...], g_ref[...]).astype(o_ref.dtype)


def _rmsnorm(x, g, out_dtype):
    n, d = x.shape
    bm = _pick(n, (256, 128, 8))
    return pl.pallas_call(
        _rmsnorm_kernel,
        grid=(n // bm,),
        in_specs=[pl.BlockSpec((bm, d), lambda i: (i, 0)),
                  pl.BlockSpec((1, d), lambda i: (0, 0))],
        out_specs=pl.BlockSpec((bm, d), lambda i: (i, 0)),
        out_shape=jax.ShapeDtypeStruct((n, d), out_dtype),
        compiler_params=_params(("parallel",), 40),
        name="rmsnorm",
    )(x, g.reshape(1, d))


FFN_ROW_CHUNK = 32


def _ffn_kernel(x_ref, gin_ref, wg_ref, wu_ref, wd_ref, gout_ref, o_ref, xn_scr, *, norm_out):
    j = pl.program_id(1)
    n_rows = x_ref.shape[0]

    def row_loop(fn):
        def trip(c, carry):
            fn(pl.ds(pl.multiple_of(c * FFN_ROW_CHUNK, FFN_ROW_CHUNK), FFN_ROW_CHUNK))
            return carry
        lax.fori_loop(0, n_rows // FFN_ROW_CHUNK, trip, 0)

    @pl.when(j == 0)
    def _():
        def prologue(rows):
            xn_scr[rows, :] = _rms(x_ref[rows, :], gin_ref[...]).astype(BF16)
            o_ref[rows, :] = jnp.zeros((FFN_ROW_CHUNK, o_ref.shape[1]), F32)
        row_loop(prologue)

    xn = xn_scr[...]
    g = jnp.dot(xn, wg_ref[...], preferred_element_type=F32)
    u = jnp.dot(xn, wu_ref[...], preferred_element_type=F32)
    a = (g * jax.nn.sigmoid(g) * u).astype(BF16)
    o_ref[...] += jnp.dot(a, wd_ref[...], preferred_element_type=F32)

    @pl.when(j == pl.num_programs(1) - 1)
    def _():
        def epilogue(rows):
            h = x_ref[rows, :] + 0.5 * o_ref[rows, :]
            o_ref[rows, :] = _rms(h, gout_ref[...]) if norm_out else h
        row_loop(epilogue)


def _ffn(x, g_in, wg, wu, wd, g_out=None):
    n, d = x.shape
    f = wg.shape[1]
    bm = _pick(n, (1024, 512, 256, 128, 32))
    bf = _pick(f, (256, 128))
    assert f % bf == 0 and bm % FFN_ROW_CHUNK == 0
    norm_out = g_out is not None
    gain = pl.BlockSpec((1, d), lambda i, j: (0, 0))
    once = pl.Buffered(1)
    return pl.pallas_call(
        functools.partial(_ffn_kernel, norm_out=norm_out),
        grid=(n // bm, f // bf),
        in_specs=[pl.BlockSpec((bm, d), lambda i, j: (i, 0), pipeline_mode=once),
                  gain,
                  pl.BlockSpec((d, bf), lambda i, j: (0, j)),
                  pl.BlockSpec((d, bf), lambda i, j: (0, j)),
                  pl.BlockSpec((bf, d), lambda i, j: (j, 0)),
                  gain],
        out_specs=pl.BlockSpec((bm, d), lambda i, j: (i, 0), pipeline_mode=once),
        out_shape=jax.ShapeDtypeStruct((n, d), F32),
        scratch_shapes=[pltpu.VMEM((bm, d), BF16)],
        compiler_params=_params(("parallel", "arbitrary"), 60),
        name="swiglu_ffn",
    )(x, g_in.reshape(1, d), wg, wu, wd, (g_out if norm_out else g_in).reshape(1, d))


def _qkv_kernel(a_ref, w_ref, c_ref, sa_ref, sb_ref, o_ref, *, rope_tiles, heads_per_tile):
    j = pl.program_id(1)
    acc = jnp.dot(a_ref[...], w_ref[...], preferred_element_type=F32)

    @pl.when(j < rope_tiles)
    def _():
        c, sa, sb = c_ref[...], sa_ref[...], sb_ref[...]
        for h in range(heads_per_tile):
            x = acc[:, h * HEAD_DIM:(h + 1) * HEAD_DIM]
            y = (x * c + pltpu.roll(x, HEAD_DIM - ROT_DIM // 2, 1) * sa
                 + pltpu.roll(x, ROT_DIM // 2, 1) * sb)
            o_ref[h] = y.astype(o_ref.dtype)

    @pl.when(j >= rope_tiles)
    def _():
        for h in range(heads_per_tile):
            o_ref[h] = acc[:, h * HEAD_DIM:(h + 1) * HEAD_DIM].astype(o_ref.dtype)


def _qkv_proj(u, w, rope, t, n_rope_cols, out_dtype):
    n, d = u.shape
    m = w.shape[1]
    bm = _pick(t, (1024, 512, 256, 128))
    bn = next(p for p in (1024, 512, 256, 128) if m % p == 0 and n_rope_cols % p == 0)
    assert n % bm == 0
    tb = t // bm
    tab = pl.BlockSpec((bm, HEAD_DIM), lambda i, j: (i % tb, 0))
    kern = functools.partial(_qkv_kernel, rope_tiles=n_rope_cols // bn, heads_per_tile=bn // HEAD_DIM)
    return pl.pallas_call(
        kern,
        grid=(n // bm, m // bn),
        in_specs=[pl.BlockSpec((bm, d), lambda i, j: (i, 0)),
                  pl.BlockSpec((d, bn), lambda i, j: (0, j)),
                  tab, tab, tab],
        out_specs=pl.BlockSpec((bn // HEAD_DIM, bm, HEAD_DIM), lambda i, j: (j, i, 0)),
        out_shape=jax.ShapeDtypeStruct((m // HEAD_DIM, n, HEAD_DIM), out_dtype),
        compiler_params=_params(("parallel", "arbitrary"), 56),
        name="qkv_proj",
    )(u, w, *rope)


def _out_proj_kernel(a_ref, w_ref, r_ref, o_ref):
    o_ref[...] = r_ref[...] + jnp.dot(a_ref[...], w_ref[...], preferred_element_type=F32)


def _out_proj(a, w, res):
    n, k = a.shape
    m = w.shape[1]
    bm = _pick(n, (1024, 512, 256, 128, 8))
    bn = _pick(m, (1024, 512, 256, 128))
    return pl.pallas_call(
        _out_proj_kernel,
        grid=(n // bm, m // bn),
        in_specs=[pl.BlockSpec((bm, k), lambda i, j: (i, 0)),
                  pl.BlockSpec((k, bn), lambda i, j: (0, j)),
                  pl.BlockSpec((bm, bn), lambda i, j: (i, j))],
        out_specs=pl.BlockSpec((bm, bn), lambda i, j: (i, j)),
        out_shape=jax.ShapeDtypeStruct((n, m), F32),
        compiler_params=_params(("parallel", "arbitrary"), 56),
        name="out_proj",
    )(a, w, res)


def _rope_tables(t):
    pos = jnp.arange(t, dtype=F32)
    inv = ROPE_THETA ** (-jnp.arange(0, ROT_DIM, 2, dtype=F32) / ROT_DIM)
    ang = pos[:, None] * inv[None, :]
    cos, sin = jnp.cos(ang), jnp.sin(ang)
    half = ROT_DIM // 2
    z_half = jnp.zeros((t, half), F32)
    z_rest = jnp.zeros((t, HEAD_DIM - ROT_DIM), F32)
    c = jnp.concatenate([cos, cos, jnp.ones((t, HEAD_DIM - ROT_DIM), F32)], axis=1)
    sa = jnp.concatenate([-sin, z_half, z_rest], axis=1)
    sb = jnp.concatenate([z_half, sin, z_rest], axis=1)
    return c, sa, sb


DIL_UNROLL = 8


def _dilated_kernel(q_ref, k_ref, v_ref, o_ref, m_scr, l_scr, mask_scr, *, t, configs):
    scale = HEAD_DIM ** -0.5
    win = 2 * QBLK
    qi = lax.broadcasted_iota(jnp.int32, (QBLK, win), 0)
    ki = lax.broadcasted_iota(jnp.int32, (QBLK, win), 1)

    halves = sorted({half for half, _ in configs})
    for hi, half in enumerate(halves):
        for vi, off in enumerate((0, half, QBLK)):
            mask_scr[3 * hi + vi] = jnp.where(jnp.abs(qi - ki + off) <= half, 0.0, NEG_INF)

    for ci, (half, d) in enumerate(configs):
        l = t // d
        nblk = l // QBLK
        first, final = ci == 0, ci == len(configs) - 1
        shift = d.bit_length() - 1
        mask0 = 3 * halves.index(half)

        def rows(r, start, size, d=d):
            if d == 1:
                return pl.ds(pl.multiple_of(start, 8), size)
            return pl.ds(r + d * start, size, stride=d)

        def scores(idx, half=half, d=d, l=l, nblk=nblk, shift=shift, rows=rows, mask0=mask0):
            i = lax.shift_right_logical(idx, shift)
            r = idx - (i << shift)
            q0 = i * QBLK
            k0 = jnp.clip(q0 - half, 0, l - win)
            rq, rk = rows(r, q0, QBLK), rows(r, k0, win)
            q = q_ref[rq, :].astype(BF16)
            k = k_ref[rk, :].astype(BF16)
            s = lax.dot_general(q, k, (((1,), (1,)), ((), ())), preferred_element_type=F32) * scale
            placement = jnp.where(i == 0, 0, jnp.where(i == nblk - 1, 2, 1))
            return rq, rk, s + mask_scr[mask0 + placement]

        def softmax(rq, rk, s):
            m = jnp.max(s, axis=-1, keepdims=True)
            p = jnp.exp(s - m)
            den = jnp.sum(p, axis=-1, keepdims=True)
            return rq, rk, m, den, p.astype(BF16)

        def values(rq, rk, m, den, p):
            v = v_ref[rk, :].astype(BF16)
            return rq, m, den, jnp.dot(p, v, preferred_element_type=F32)

        def merge(rq, m, den, acc, first=first, final=final):
            if first:
                o_ref[rq, :] = acc
                m_scr[rq, :] = jnp.broadcast_to(m, (QBLK, LANES))
                l_scr[rq, :] = jnp.broadcast_to(den, (QBLK, LANES))
                return
            m_run = m_scr[rq, :]
            m_new = jnp.maximum(m_run, m)
            a_run = jnp.exp(m_run - m_new)
            a_blk = jnp.exp(m - m_new)
            num = o_ref[rq, :] * a_run + acc * a_blk
            tot = l_scr[rq, :] * a_run + den * a_blk
            if final:
                o_ref[rq, :] = num / tot
            else:
                o_ref[rq, :] = num
                m_scr[rq, :] = m_new
                l_scr[rq, :] = tot

        total = d * nblk
        unroll = DIL_UNROLL if total % DIL_UNROLL == 0 else 1

        def trip(it, carry, scores=scores, softmax=softmax, values=values, merge=merge, unroll=unroll):
            parts = [scores(it * unroll + u) for u in range(unroll)]
            parts = [softmax(*part) for part in parts]
            parts = [values(*part) for part in parts]
            for part in parts:
                merge(*part)
            return carry

        lax.fori_loop(0, total // unroll, trip, 0)


def _dilated_attention(qkv_a, b, t, n_heads):
    n = b * t
    nh = n_heads
    configs = tuple(sorted(((window // (2 * dil), dil) for window, dil in DILATED_CONFIGS),
                           key=lambda c: -c[1]))
    n_masks = 3 * len({half for half, _ in configs})
    for half, d in configs:
        assert d & (d - 1) == 0 and t % (d * QBLK) == 0 and t // d >= 2 * QBLK and half <= QBLK // 2
    blk = (None, t, HEAD_DIM)
    return pl.pallas_call(
        functools.partial(_dilated_kernel, t=t, configs=configs),
        grid=(b, nh),
        in_specs=[pl.BlockSpec(blk, lambda bi, h: (h, bi, 0)),
                  pl.BlockSpec(blk, lambda bi, h: (nh + h, bi, 0)),
                  pl.BlockSpec(blk, lambda bi, h: (2 * nh + h, bi, 0))],
        out_specs=pl.BlockSpec((t, HEAD_DIM), lambda bi, h: (bi, h)),
        out_shape=jax.ShapeDtypeStruct((n, nh * HEAD_DIM), F32),
        scratch_shapes=[pltpu.VMEM((t, LANES), F32), pltpu.VMEM((t, LANES), F32),
                        pltpu.VMEM((n_masks, QBLK, 2 * QBLK), F32)],
        compiler_params=_params(("parallel", "arbitrary"), 56),
        name="dilated_attn",
    )(qkv_a, qkv_a, qkv_a)


NBR_UNROLL = 8


def _nbr_kernel(q_ref, k_ref, v_ref, bias_ref, o_ref, *, rows):
    scale = HEAD_DIM ** -0.5
    wk = WIN_ROWS * GRID_W

    def scores(r):
        rs = jnp.clip(r - WIN_ROWS // 2, 0, rows - WIN_ROWS)
        q0 = pl.multiple_of(r * GRID_W, GRID_W)
        k0 = pl.multiple_of(rs * GRID_W, GRID_W)
        q = q_ref[pl.ds(q0, GRID_W), :]
        k = k_ref[pl.ds(k0, wk), :]
        s = lax.dot_general(q, k, (((1,), (1,)), ((), ())), preferred_element_type=F32) * scale
        return q0, k0, s + bias_ref[rs - r + (WIN_ROWS - 1)]

    def softmax(q0, k0, s):
        m = jnp.max(s, axis=-1, keepdims=True)
        p = jnp.exp(s - m)
        return q0, k0, jnp.sum(p, axis=-1, keepdims=True), p.astype(BF16)

    def values(q0, k0, den, p):
        o = jnp.dot(p, v_ref[pl.ds(k0, wk), :], preferred_element_type=F32)
        o_ref[pl.ds(q0, GRID_W), :] = o / den

    unroll = NBR_UNROLL if rows % NBR_UNROLL == 0 else 1

    def trip(it, carry):
        parts = [scores(it * unroll + u) for u in range(unroll)]
        parts = [softmax(*part) for part in parts]
        for part in parts:
            values(*part)
        return carry

    lax.fori_loop(0, rows // unroll, trip, 0)


def _nbr_bias_tables(rel_bias):
    nh = rel_bias.shape[0]
    c = jnp.arange(GRID_W)
    cs = jnp.clip(c - WIN_COLS // 2, 0, GRID_W - WIN_COLS)
    col_mask = (c[None, :] >= cs[:, None]) & (c[None, :] < cs[:, None] + WIN_COLS)
    dc = jnp.clip(c[None, :] - c[:, None], -(WIN_COLS - 1), WIN_COLS - 1) + (WIN_COLS - 1)
    tab = rel_bias.astype(F32)[:, :, dc]
    tab = jnp.where(col_mask[None, None], tab, NEG_INF)
    idx = jnp.arange(WIN_ROWS)[:, None] + jnp.arange(WIN_ROWS)[None, :]
    tab = tab[:, idx]
    return tab.transpose(0, 1, 3, 2, 4).reshape(nh, WIN_ROWS, GRID_W, WIN_ROWS * GRID_W)


def _nbr_attention(qkvh, bias_tab, b, t, head0, n_heads):
    n = b * t
    rows = t // GRID_W
    assert t % GRID_W == 0 and rows >= WIN_ROWS
    nh = n_heads
    blk = (None, t, HEAD_DIM)
    kern = functools.partial(_nbr_kernel, rows=rows)
    return pl.pallas_call(
        kern,
        grid=(b, nh),
        in_specs=[pl.BlockSpec(blk, lambda bi, h: (head0 + h, bi, 0)),
                  pl.BlockSpec(blk, lambda bi, h: (head0 + nh + h, bi, 0)),
                  pl.BlockSpec(blk, lambda bi, h: (head0 + 2 * nh + h, bi, 0)),
                  pl.BlockSpec((None, WIN_ROWS, GRID_W, WIN_ROWS * GRID_W), lambda bi, h: (h, 0, 0, 0))],
        out_specs=pl.BlockSpec((t, HEAD_DIM), lambda bi, h: (bi, h)),
        out_shape=jax.ShapeDtypeStruct((n, nh * HEAD_DIM), F32),
        compiler_params=_params(("parallel", "arbitrary"), 48),
        name="nbr_attn",
    )(qkvh, qkvh, qkvh, bias_tab)


def _out_norms_kernel(oa_ref, ob_ref, ga_ref, gb_ref, out_ref):
    wa = oa_ref.shape[1]
    out_ref[:, :wa] = _rms(oa_ref[...], ga_ref[...]).astype(out_ref.dtype)
    out_ref[:, wa:] = _rms(ob_ref[...], gb_ref[...]).astype(out_ref.dtype)


def _out_norms(oa, ob, ga, gb):
    n, wa = oa.shape
    wb = ob.shape[1]
    bm = _pick(n, (256, 128, 8))
    return pl.pallas_call(
        _out_norms_kernel,
        grid=(n // bm,),
        in_specs=[pl.BlockSpec((bm, wa), lambda i: (i, 0)),
                  pl.BlockSpec((bm, wb), lambda i: (i, 0)),
                  pl.BlockSpec((1, wa), lambda i: (0, 0)),
                  pl.BlockSpec((1, wb), lambda i: (0, 0))],
        out_specs=pl.BlockSpec((bm, wa + wb), lambda i: (i, 0)),
        out_shape=jax.ShapeDtypeStruct((n, wa + wb), BF16),
        compiler_params=_params(("parallel",), 40),
        name="out_norms",
    )(oa, ob, ga.reshape(1, wa), gb.reshape(1, wb))


def _prep_ffn(w_gate, w_up, w_down):
    return w_gate.astype(BF16), w_up.astype(BF16), w_down.astype(BF16)


def _layer(x, b, t, wts, final_gain):
    (ffn1_norm, ffn1, mix_norm, w_in_a, w_in_b, bias_tab, out_norm_a, out_norm_b, w_out, ffn2_norm,
     ffn2) = wts
    wa = out_norm_a.shape[0]
    wb = out_norm_b.shape[0]
    nha, nhb = wa // HEAD_DIM, wb // HEAD_DIM
    rope = _rope_tables(t)

    h1 = _ffn(x, ffn1_norm, *ffn1)
    u = _rmsnorm(h1, mix_norm, BF16)

    qkv_a = _qkv_proj(u, w_in_a, rope, t, 2 * wa, F32)
    qkv_b = _qkv_proj(u, w_in_b, rope, t, 0, BF16)
    oa = _dilated_attention(qkv_a, b, t, nha)
    ob = _nbr_attention(qkv_b, bias_tab, b, t, 0, nhb)
    merged = _out_norms(oa, ob, out_norm_a, out_norm_b)
    h2 = _out_proj(merged, w_out, h1)

    return _ffn(h2, ffn2_norm, *ffn2, g_out=final_gain)


def kernel(x_prompt, x_sample, ffn1_norm, ffn1_w_gate, ffn1_w_up, ffn1_w_down, mix_norm, w_in, nbr_rel_bias, out_norm_a, out_norm_b, w_out, ffn2_norm, ffn2_w_gate, ffn2_w_up, ffn2_w_down, final_norm):
    depth = ffn1_norm.shape[0]
    layers = []
    for i in range(depth):
        cols_a = 3 * out_norm_a.shape[1]
        layers.append((ffn1_norm[i],
                       _prep_ffn(ffn1_w_gate[i], ffn1_w_up[i], ffn1_w_down[i]),
                       mix_norm[i], w_in[i, :, :cols_a].astype(BF16), w_in[i, :, cols_a:].astype(BF16),
                       _nbr_bias_tables(nbr_rel_bias[i]),
                       out_norm_a[i], out_norm_b[i], w_out[i].astype(BF16), ffn2_norm[i],
                       _prep_ffn(ffn2_w_gate[i], ffn2_w_up[i], ffn2_w_down[i])))

    def run(x3):
        b, t, d = x3.shape
        h = x3.reshape(b * t, d)
        for i in range(depth):
            h = _layer(h, b, t, layers[i], final_norm if i == depth - 1 else None)
        return h.reshape(b, t, d)

    return run(x_prompt), run(x_sample)
```

```python
import functools

import jax
import jax.numpy as jnp
from jax import lax
from jax.experimental import pallas as pl
from jax.experimental.pallas import tpu as pltpu

HEAD_DIM = 128
DILATED_CONFIGS = ((128, 1), (512, 4), (2048, 16))
QBLK = 128
ROPE_THETA = 500000.0
ROT_DIM = HEAD_DIM // 4
GRID_W = 64
WIN_ROWS = 8
WIN_COLS = 16
NORM_EPS = 1e-6
NEG_INF = -1e30
LANES = 128
V7X_VMEM_BYTES = 64 * 1024 * 1024

F32 = jnp.float32
BF16 = jnp.bfloat16


def _params(sem, vmem_mb):
    return pltpu.CompilerParams(dimension_semantics=sem,
                                vmem_limit_bytes=min(vmem_mb * 1024 * 1024, V7X_VMEM_BYTES - (2 << 20)))


def _pick(n, prefs):
    for p in prefs:
        if n % p == 0:
            return p
    return n


def _cast_kernel(w_ref, o_ref):
    o_ref[...] = w_ref[...].astype(o_ref.dtype)


def _cast_bf16(w, layer):
    _, r, c = w.shape
    br = _pick(r, (256, 128, 16))
    return pl.pallas_call(
        _cast_kernel,
        grid=(r // br,),
        in_specs=[pl.BlockSpec((None, br, c), lambda i: (layer, i, 0))],
        out_specs=pl.BlockSpec((br, c), lambda i: (i, 0)),
        out_shape=jax.ShapeDtypeStruct((r, c), BF16),
        compiler_params=_params(("parallel",), 56),
        name="cast_bf16",
    )(w)


def _rms(x, g):
    ms = jnp.mean(x * x, axis=-1, keepdims=True)
    return x * lax.rsqrt(ms + NORM_EPS) * g


def _rmsnorm_kernel(x_ref, g_ref, o_ref):
    o_ref[...] = _rms(x_ref[...], g_ref[...]).astype(o_ref.dtype)


def _rmsnorm(x, g, out_dtype):
    n, d = x.shape
    bm = _pick(n, (256, 128, 8))
    return pl.pallas_call(
        _rmsnorm_kernel,
        grid=(n // bm,),
        in_specs=[pl.BlockSpec((bm, d), lambda i: (i, 0)),
                  pl.BlockSpec((1, d), lambda i: (0, 0))],
        out_specs=pl.BlockSpec((bm, d), lambda i: (i, 0)),
        out_shape=jax.ShapeDtypeStruct((n, d), out_dtype),
        compiler_params=_params(("parallel",), 40),
        name="rmsnorm",
    )(x, g.reshape(1, d))


FFN_ROW_CHUNK = 32


def _ffn_kernel(x_ref, gin_ref, wg_ref, wu_ref, wd_ref, gout_ref, o_ref, xn_scr, *, norm_out):
    j = pl.program_id(1)
    n_rows, width = x_ref.shape
    slabs = [slice(c, c + LANES) for c in range(0, width, LANES)]

    def row_loop(fn):
        def trip(c, carry):
            fn(pl.ds(pl.multiple_of(c * FFN_ROW_CHUNK, FFN_ROW_CHUNK), FFN_ROW_CHUNK))
            return carry
        lax.fori_loop(0, n_rows // FFN_ROW_CHUNK, trip, 0, unroll=2)

    def inv_rms(sq_sum):
        return lax.rsqrt(jnp.sum(sq_sum, axis=-1, keepdims=True) / width + NORM_EPS)

    @pl.when(j == 0)
    def _():
        def prologue(rows):
            sq = jnp.zeros((FFN_ROW_CHUNK, LANES), F32)
            for sl in slabs:
                v = x_ref[rows, sl]
                sq = sq + v * v
                o_ref[rows, sl] = jnp.zeros((FFN_ROW_CHUNK, LANES), F32)
            inv = inv_rms(sq)
            for sl in slabs:
                xn_scr[rows, sl] = (x_ref[rows, sl] * inv * gin_ref[:, sl]).astype(BF16)
        row_loop(prologue)

    xn = xn_scr[...]
    g = jnp.dot(xn, wg_ref[...], preferred_element_type=F32)
    u = jnp.dot(xn, wu_ref[...], preferred_element_type=F32)
    a = (g * jax.nn.sigmoid(g) * u).astype(BF16)
    o_ref[...] += jnp.dot(a, wd_ref[...], preferred_element_type=F32)

    @pl.when(j == pl.num_programs(1) - 1)
    def _():
        def epilogue(rows):
            sq = jnp.zeros((FFN_ROW_CHUNK, LANES), F32)
            for sl in slabs:
                h = x_ref[rows, sl] + 0.5 * o_ref[rows, sl]
                o_ref[rows, sl] = h
                if norm_out:
                    sq = sq + h * h
            if norm_out:
                inv = inv_rms(sq)
                for sl in slabs:
                    o_ref[rows, sl] = o_ref[rows, sl] * inv * gout_ref[:, sl]
        row_loop(epilogue)


def _ffn(x, g_in, wg, wu, wd, g_out=None):
    n, d = x.shape
    f = wg.shape[1]
    bm = _pick(n, (1024, 512, 256, 128, 32))
    bf = _pick(f, (256, 128))
    assert f % bf == 0 and bm % FFN_ROW_CHUNK == 0
    norm_out = g_out is not None
    gain = pl.BlockSpec((1, d), lambda i, j: (0, 0))
    once = pl.Buffered(1)
    return pl.pallas_call(
        functools.partial(_ffn_kernel, norm_out=norm_out),
        grid=(n // bm, f // bf),
        in_specs=[pl.BlockSpec((bm, d), lambda i, j: (i, 0), pipeline_mode=once),
                  gain,
                  pl.BlockSpec((d, bf), lambda i, j: (0, j)),
                  pl.BlockSpec((d, bf), lambda i, j: (0, j)),
                  pl.BlockSpec((bf, d), lambda i, j: (j, 0)),
                  gain],
        out_specs=pl.BlockSpec((bm, d), lambda i, j: (i, 0), pipeline_mode=once),
        out_shape=jax.ShapeDtypeStruct((n, d), F32),
        scratch_shapes=[pltpu.VMEM((bm, d), BF16)],
        compiler_params=_params(("parallel", "arbitrary"), 60),
        name="swiglu_ffn",
    )(x, g_in.reshape(1, d), wg, wu, wd, (g_out if norm_out else g_in).reshape(1, d))


def _qkv_kernel(a_ref, w_ref, *refs, rotary, heads_per_tile):
    o_ref = refs[-1]
    acc = jnp.dot(a_ref[...], w_ref[...], preferred_element_type=F32)
    if rotary:
        c, sa, sb = (r[...] for r in refs[:3])
    for h in range(heads_per_tile):
        x = acc[:, h * HEAD_DIM:(h + 1) * HEAD_DIM]
        if rotary:
            x = (x * c + pltpu.roll(x, HEAD_DIM - ROT_DIM // 2, 1) * sa
                 + pltpu.roll(x, ROT_DIM // 2, 1) * sb)
        o_ref[h] = x.astype(o_ref.dtype)


def _qkv_proj(u, w, col0, m, rope, t, n_rope_cols, out_dtype):
    n, d = u.shape
    bm = _pick(t, (1024, 512, 256, 128))
    bn = next(p for p in (1024, 512, 256, 128)
              if m % p == 0 and n_rope_cols % p == 0 and col0 % p == 0)
    assert n % bm == 0
    tb = t // bm
    j0 = col0 // bn
    rope_tiles = n_rope_cols // bn
    rotary = rope_tiles > 0
    tab = pl.BlockSpec((None, bm, HEAD_DIM), lambda i, j: (jnp.where(j < rope_tiles, 0, 1), i % tb, 0))
    kern = functools.partial(_qkv_kernel, rotary=rotary, heads_per_tile=bn // HEAD_DIM)
    return pl.pallas_call(
        kern,
        grid=(n // bm, m // bn),
        in_specs=[pl.BlockSpec((bm, d), lambda i, j: (i, 0)),
                  pl.BlockSpec((d, bn), lambda i, j: (0, j0 + j))] + ([tab, tab, tab] if rotary else []),
        out_specs=pl.BlockSpec((bn // HEAD_DIM, bm, HEAD_DIM), lambda i, j: (j, i, 0)),
        out_shape=jax.ShapeDtypeStruct((m // HEAD_DIM, n, HEAD_DIM), out_dtype),
        compiler_params=_params(("parallel", "arbitrary"), 56),
        name="qkv_proj",
    )(u, w, *(rope if rotary else ()))


def _out_proj_kernel(a_ref, w_ref, r_ref, o_ref):
    o_ref[...] = r_ref[...] + jnp.dot(a_ref[...], w_ref[...], preferred_element_type=F32)


def _out_proj(a, w, res):
    n, k = a.shape
    m = w.shape[1]
    bm = _pick(n, (1024, 512, 256, 128, 8))
    bn = _pick(m, (1024, 512, 256, 128))
    return pl.pallas_call(
        _out_proj_kernel,
        grid=(n // bm, m // bn),
        in_specs=[pl.BlockSpec((bm, k), lambda i, j: (i, 0)),
                  pl.BlockSpec((k, bn), lambda i, j: (0, j)),
                  pl.BlockSpec((bm, bn), lambda i, j: (i, j))],
        out_specs=pl.BlockSpec((bm, bn), lambda i, j: (i, j)),
        out_shape=jax.ShapeDtypeStruct((n, m), F32),
        compiler_params=_params(("parallel", "arbitrary"), 56),
        name="out_proj",
    )(a, w, res)


def _rope_tables(t):
    pos = jnp.arange(t, dtype=F32)
    inv = ROPE_THETA ** (-jnp.arange(0, ROT_DIM, 2, dtype=F32) / ROT_DIM)
    ang = pos[:, None] * inv[None, :]
    cos, sin = jnp.cos(ang), jnp.sin(ang)
    half = ROT_DIM // 2
    z_half = jnp.zeros((t, half), F32)
    z_rest = jnp.zeros((t, HEAD_DIM - ROT_DIM), F32)
    c = jnp.concatenate([cos, cos, jnp.ones((t, HEAD_DIM - ROT_DIM), F32)], axis=1)
    sa = jnp.concatenate([-sin, z_half, z_rest], axis=1)
    sb = jnp.concatenate([z_half, sin, z_rest], axis=1)
    return (jnp.stack([c, jnp.ones_like(c)]), jnp.stack([sa, jnp.zeros_like(sa)]),
            jnp.stack([sb, jnp.zeros_like(sb)]))


DIL_UNROLL = 8


def _dilated_kernel(q_ref, k_ref, v_ref, o_ref, m_scr, l_scr, mask_scr, *, t, configs):
    scale = HEAD_DIM ** -0.5
    win = 2 * QBLK
    qi = lax.broadcasted_iota(jnp.int32, (QBLK, win), 0)
    ki = lax.broadcasted_iota(jnp.int32, (QBLK, win), 1)

    halves = sorted({half for half, _ in configs})
    for hi, half in enumerate(halves):
        for vi, off in enumerate((0, half, QBLK)):
            mask_scr[3 * hi + vi] = jnp.where(jnp.abs(qi - ki + off) <= half, 0.0, NEG_INF)

    for ci, (half, d) in enumerate(configs):
        l = t // d
        nblk = l // QBLK
        first, final = ci == 0, ci == len(configs) - 1
        shift = d.bit_length() - 1
        mask0 = 3 * halves.index(half)

        def rows(r, start, size, d=d):
            if d == 1:
                return pl.ds(pl.multiple_of(start, 8), size)
            return pl.ds(r + d * start, size, stride=d)

        def scores(idx, half=half, d=d, l=l, nblk=nblk, shift=shift, rows=rows, mask0=mask0):
            i = lax.shift_right_logical(idx, shift)
            r = idx - (i << shift)
            q0 = i * QBLK
            k0 = jnp.clip(q0 - half, 0, l - win)
            rq, rk = rows(r, q0, QBLK), rows(r, k0, win)
            q = q_ref[rq, :].astype(BF16)
            k = k_ref[rk, :].astype(BF16)
            s = lax.dot_general(q, k, (((1,), (1,)), ((), ())), preferred_element_type=F32) * scale
            placement = jnp.where(i == 0, 0, jnp.where(i == nblk - 1, 2, 1))
            return rq, rk, s + mask_scr[mask0 + placement]

        def softmax(rq, rk, s):
            m = jnp.max(s, axis=-1, keepdims=True)
            p = jnp.exp(s - m)
            den = jnp.sum(p, axis=-1, keepdims=True)
            return rq, rk, m, den, p.astype(BF16)

        def values(rq, rk, m, den, p):
            v = v_ref[rk, :].astype(BF16)
            return rq, m, den, jnp.dot(p, v, preferred_element_type=F32)

        def merge(rq, m, den, acc, first=first, final=final):
            if first:
                o_ref[rq, :] = acc
                m_scr[rq, :] = jnp.broadcast_to(m, (QBLK, LANES))
                l_scr[rq, :] = jnp.broadcast_to(den, (QBLK, LANES))
                return
            m_run = m_scr[rq, :]
            m_new = jnp.maximum(m_run, m)
            a_run = jnp.exp(m_run - m_new)
            a_blk = jnp.exp(m - m_new)
            num = o_ref[rq, :] * a_run + acc * a_blk
            tot = l_scr[rq, :] * a_run + den * a_blk
            if final:
                o_ref[rq, :] = num / tot
            else:
                o_ref[rq, :] = num
                m_scr[rq, :] = m_new
                l_scr[rq, :] = tot

        total = d * nblk
        unroll = DIL_UNROLL if total % DIL_UNROLL == 0 else 1

        def trip(it, carry, scores=scores, softmax=softmax, values=values, merge=merge, unroll=unroll):
            parts = [scores(it * unroll + u) for u in range(unroll)]
            parts = [softmax(*part) for part in parts]
            parts = [values(*part) for part in parts]
            for part in parts:
                merge(*part)
            return carry

        lax.fori_loop(0, total // unroll, trip, 0)


def _dilated_attention(qkv_a, b, t, n_heads):
    n = b * t
    nh = n_heads
    configs = tuple(sorted(((window // (2 * dil), dil) for window, dil in DILATED_CONFIGS),
                           key=lambda c: -c[1]))
    n_masks = 3 * len({half for half, _ in configs})
    for half, d in configs:
        assert d & (d - 1) == 0 and t % (d * QBLK) == 0 and t // d >= 2 * QBLK and half <= QBLK // 2
    blk = (None, t, HEAD_DIM)
    return pl.pallas_call(
        functools.partial(_dilated_kernel, t=t, configs=configs),
        grid=(b, nh),
        in_specs=[pl.BlockSpec(blk, lambda bi, h: (h, bi, 0)),
                  pl.BlockSpec(blk, lambda bi, h: (nh + h, bi, 0)),
                  pl.BlockSpec(blk, lambda bi, h: (2 * nh + h, bi, 0))],
        out_specs=pl.BlockSpec((t, HEAD_DIM), lambda bi, h: (bi, h)),
        out_shape=jax.ShapeDtypeStruct((n, nh * HEAD_DIM), F32),
        scratch_shapes=[pltpu.VMEM((t, LANES), F32), pltpu.VMEM((t, LANES), F32),
                        pltpu.VMEM((n_masks, QBLK, 2 * QBLK), F32)],
        compiler_params=_params(("parallel", "arbitrary"), 56),
        name="dilated_attn",
    )(qkv_a, qkv_a, qkv_a)


NBR_UNROLL = 8


def _nbr_kernel(q_ref, k_ref, v_ref, bias_ref, o_ref, *, rows):
    scale = HEAD_DIM ** -0.5
    wk = WIN_ROWS * GRID_W

    def scores(r):
        rs = jnp.clip(r - WIN_ROWS // 2, 0, rows - WIN_ROWS)
        q0 = pl.multiple_of(r * GRID_W, GRID_W)
        k0 = pl.multiple_of(rs * GRID_W, GRID_W)
        q = q_ref[pl.ds(q0, GRID_W), :]
        k = k_ref[pl.ds(k0, wk), :]
        s = lax.dot_general(q, k, (((1,), (1,)), ((), ())), preferred_element_type=F32) * scale
        return q0, k0, s + bias_ref[rs - r + (WIN_ROWS - 1)]

    def softmax(q0, k0, s):
        m = jnp.max(s, axis=-1, keepdims=True)
        p = jnp.exp(s - m)
        return q0, k0, jnp.sum(p, axis=-1, keepdims=True), p.astype(BF16)

    def values(q0, k0, den, p):
        o = jnp.dot(p, v_ref[pl.ds(k0, wk), :], preferred_element_type=F32)
        o_ref[pl.ds(q0, GRID_W), :] = o / den

    unroll = NBR_UNROLL if rows % NBR_UNROLL == 0 else 1

    def trip(it, carry):
        parts = [scores(it * unroll + u) for u in range(unroll)]
        parts = [softmax(*part) for part in parts]
        for part in parts:
            values(*part)
        return carry

    lax.fori_loop(0, rows // unroll, trip, 0)


def _nbr_bias_tables(rel_bias):
    nh = rel_bias.shape[0]
    c = jnp.arange(GRID_W)
    cs = jnp.clip(c - WIN_COLS // 2, 0, GRID_W - WIN_COLS)
    col_mask = (c[None, :] >= cs[:, None]) & (c[None, :] < cs[:, None] + WIN_COLS)
    dc = jnp.clip(c[None, :] - c[:, None], -(WIN_COLS - 1), WIN_COLS - 1) + (WIN_COLS - 1)
    tab = rel_bias.astype(F32)[:, :, dc]
    tab = jnp.where(col_mask[None, None], tab, NEG_INF)
    idx = jnp.arange(WIN_ROWS)[:, None] + jnp.arange(WIN_ROWS)[None, :]
    tab = tab[:, idx]
    return tab.transpose(0, 1, 3, 2, 4).reshape(nh, WIN_ROWS, GRID_W, WIN_ROWS * GRID_W)


def _nbr_attention(qkvh, bias_tab, b, t, head0, n_heads):
    n = b * t
    rows = t // GRID_W
    assert t % GRID_W == 0 and rows >= WIN_ROWS
    nh = n_heads
    blk = (None, t, HEAD_DIM)
    kern = functools.partial(_nbr_kernel, rows=rows)
    return pl.pallas_call(
        kern,
        grid=(b, nh),
        in_specs=[pl.BlockSpec(blk, lambda bi, h: (head0 + h, bi, 0)),
                  pl.BlockSpec(blk, lambda bi, h: (head0 + nh + h, bi, 0)),
                  pl.BlockSpec(blk, lambda bi, h: (head0 + 2 * nh + h, bi, 0)),
                  pl.BlockSpec((None, WIN_ROWS, GRID_W, WIN_ROWS * GRID_W), lambda bi, h: (h, 0, 0, 0))],
        out_specs=pl.BlockSpec((t, HEAD_DIM), lambda bi, h: (bi, h)),
        out_shape=jax.ShapeDtypeStruct((n, nh * HEAD_DIM), F32),
        compiler_params=_params(("parallel", "arbitrary"), 48),
        name="nbr_attn",
    )(qkvh, qkvh, qkvh, bias_tab)


def _out_norms_kernel(oa_ref, ob_ref, ga_ref, gb_ref, out_ref):
    wa = oa_ref.shape[1]
    out_ref[:, :wa] = _rms(oa_ref[...], ga_ref[...]).astype(out_ref.dtype)
    out_ref[:, wa:] = _rms(ob_ref[...], gb_ref[...]).astype(out_ref.dtype)


def _out_norms(oa, ob, ga, gb):
    n, wa = oa.shape
    wb = ob.shape[1]
    bm = _pick(n, (256, 128, 8))
    return pl.pallas_call(
        _out_norms_kernel,
        grid=(n // bm,),
        in_specs=[pl.BlockSpec((bm, wa), lambda i: (i, 0)),
                  pl.BlockSpec((bm, wb), lambda i: (i, 0)),
                  pl.BlockSpec((1, wa), lambda i: (0, 0)),
                  pl.BlockSpec((1, wb), lambda i: (0, 0))],
        out_specs=pl.BlockSpec((bm, wa + wb), lambda i: (i, 0)),
        out_shape=jax.ShapeDtypeStruct((n, wa + wb), BF16),
        compiler_params=_params(("parallel",), 40),
        name="out_norms",
    )(oa, ob, ga.reshape(1, wa), gb.reshape(1, wb))


def _layer(x, b, t, wts, final_gain):
    (ffn1_norm, ffn1, mix_norm, w_in, bias_tab, out_norm_a, out_norm_b, w_out, ffn2_norm, ffn2) = wts
    wa = out_norm_a.shape[0]
    wb = out_norm_b.shape[0]
    nha, nhb = wa // HEAD_DIM, wb // HEAD_DIM
    rope = _rope_tables(t)

    h1 = _ffn(x, ffn1_norm, *ffn1)
    u = _rmsnorm(h1, mix_norm, BF16)

    qkv_a = _qkv_proj(u, w_in, 0, 3 * wa, rope, t, 2 * wa, F32)
    qkv_b = _qkv_proj(u, w_in, 3 * wa, 3 * wb, rope, t, 0, BF16)
    oa = _dilated_attention(qkv_a, b, t, nha)
    ob = _nbr_attention(qkv_b, bias_tab, b, t, 0, nhb)
    merged = _out_norms(oa, ob, out_norm_a, out_norm_b)
    h2 = _out_proj(merged, w_out, h1)

    return _ffn(h2, ffn2_norm, *ffn2, g_out=final_gain)


def kernel(x_prompt, x_sample, ffn1_norm, ffn1_w_gate, ffn1_w_up, ffn1_w_down, mix_norm, w_in, nbr_rel_bias, out_norm_a, out_norm_b, w_out, ffn2_norm, ffn2_w_gate, ffn2_w_up, ffn2_w_down, final_norm):
    depth = ffn1_norm.shape[0]
    layers = []
    for i in range(depth):
        layers.append((ffn1_norm[i],
                       tuple(_cast_bf16(w, i) for w in (ffn1_w_gate, ffn1_w_up, ffn1_w_down)),
                       mix_norm[i], _cast_bf16(w_in, i), _nbr_bias_tables(nbr_rel_bias[i]),
                       out_norm_a[i], out_norm_b[i], _cast_bf16(w_out, i), ffn2_norm[i],
                       tuple(_cast_bf16(w, i) for w in (ffn2_w_gate, ffn2_w_up, ffn2_w_down))))

    def run(x3):
        b, t, d = x3.shape
        h = x3.reshape(b * t, d)
        for i in range(depth):
            h = _layer(h, b, t, layers[i], final_norm if i == depth - 1 else None)
        return h.reshape(b, t, d)

    return run(x_prompt), run(x_sample)
```

```python
import functools

import jax
import jax.numpy as jnp
from jax import lax
from jax.experimental import pallas as pl
from jax.experimental.pallas import tpu as pltpu

HEAD_DIM = 128
DILATED_CONFIGS = ((128, 1), (512, 4), (2048, 16))
QBLK = 128
ROPE_THETA = 500000.0
ROT_DIM = HEAD_DIM // 4
GRID_W = 64
WIN_ROWS = 8
WIN_COLS = 16
NORM_EPS = 1e-6
NEG_INF = -1e30
LANES = 128
V7X_VMEM_BYTES = 64 * 1024 * 1024

F32 = jnp.float32
BF16 = jnp.bfloat16


def _params(sem, vmem_mb):
    return pltpu.CompilerParams(dimension_semantics=sem,
                                vmem_limit_bytes=min(vmem_mb * 1024 * 1024, V7X_VMEM_BYTES - (2 << 20)))


def _pick(n, prefs):
    for p in prefs:
        if n % p == 0:
            return p
    return n


def _cast_kernel(w_ref, o_ref):
    o_ref[...] = w_ref[...].astype(o_ref.dtype)


def _cast_bf16(w, layer):
    _, r, c = w.shape
    br = _pick(r, (256, 128, 16))
    return pl.pallas_call(
        _cast_kernel,
        grid=(r // br,),
        in_specs=[pl.BlockSpec((None, br, c), lambda i: (layer, i, 0))],
        out_specs=pl.BlockSpec((br, c), lambda i: (i, 0)),
        out_shape=jax.ShapeDtypeStruct((r, c), BF16),
        compiler_params=_params(("parallel",), 56),
        name="cast_bf16",
    )(w)


def _rms(x, g):
    ms = jnp.mean(x * x, axis=-1, keepdims=True)
    return x * lax.rsqrt(ms + NORM_EPS) * g


def _rmsnorm_kernel(x_ref, g_ref, o_ref):
    o_ref[...] = _rms(x_ref[...], g_ref[...]).astype(o_ref.dtype)


def _rmsnorm(x, g, out_dtype):
    n, d = x.shape
    bm = _pick(n, (256, 128, 8))
    return pl.pallas_call(
        _rmsnorm_kernel,
        grid=(n // bm,),
        in_specs=[pl.BlockSpec((bm, d), lambda i: (i, 0)),
                  pl.BlockSpec((1, d), lambda i: (0, 0))],
        out_specs=pl.BlockSpec((bm, d), lambda i: (i, 0)),
        out_shape=jax.ShapeDtypeStruct((n, d), out_dtype),
        compiler_params=_params(("parallel",), 40),
        name="rmsnorm",
    )(x, g.reshape(1, d))


FFN_ROW_CHUNK = 32


def _ffn_kernel(x_hbm, gin_ref, wg_ref, wu_ref, wd_ref, gout_ref, o_hbm, acc, xn_scr, in_sem, out_sem,
                *, norm_out):
    i, j = pl.program_id(0), pl.program_id(1)
    n_tiles, n_chunks = pl.num_programs(0), pl.num_programs(1)
    _, n_rows, width = acc.shape
    slabs = [slice(c, c + LANES) for c in range(0, width, LANES)]
    slot = lax.rem(i, 2)
    tile = acc.at[slot]

    def x_copy(t, s):
        return pltpu.make_async_copy(x_hbm.at[pl.ds(t * n_rows, n_rows), :], acc.at[s], in_sem.at[s])

    def o_copy(t, s):
        return pltpu.make_async_copy(acc.at[s], o_hbm.at[pl.ds(t * n_rows, n_rows), :], out_sem.at[s])

    def row_loop(fn):
        def trip(c, carry):
            fn(pl.ds(pl.multiple_of(c * FFN_ROW_CHUNK, FFN_ROW_CHUNK), FFN_ROW_CHUNK))
            return carry
        lax.fori_loop(0, n_rows // FFN_ROW_CHUNK, trip, 0, unroll=2)

    def inv_rms(sq_sum):
        return lax.rsqrt(jnp.sum(sq_sum, axis=-1, keepdims=True) / width + NORM_EPS)

    @pl.when(j == 0)
    def _():
        @pl.when(i == 0)
        def _():
            x_copy(0, 0).start()
        x_copy(i, slot).wait()

        def prologue(rows):
            sq = jnp.zeros((FFN_ROW_CHUNK, LANES), F32)
            for sl in slabs:
                v = tile[rows, sl]
                sq = sq + v * v
            inv = inv_rms(sq)
            for sl in slabs:
                v = tile[rows, sl]
                xn_scr[rows, sl] = (v * inv * gin_ref[:, sl]).astype(BF16)
                tile[rows, sl] = v + v
        row_loop(prologue)

    xn = xn_scr[...]
    g = jnp.dot(xn, wg_ref[...], preferred_element_type=F32)
    u = jnp.dot(xn, wu_ref[...], preferred_element_type=F32)
    a = (g * jax.nn.sigmoid(g) * u).astype(BF16)
    tile[...] += jnp.dot(a, wd_ref[...], preferred_element_type=F32)

    @pl.when(j == 1)
    def _():
        @pl.when(i >= 1)
        def _():
            o_copy(i - 1, 1 - slot).wait()

        @pl.when(i + 1 < n_tiles)
        def _():
            x_copy(i + 1, 1 - slot).start()

    @pl.when(j == n_chunks - 1)
    def _():
        def epilogue(rows):
            sq = jnp.zeros((FFN_ROW_CHUNK, LANES), F32)
            for sl in slabs:
                h = 0.5 * tile[rows, sl]
                tile[rows, sl] = h
                if norm_out:
                    sq = sq + h * h
            if norm_out:
                inv = inv_rms(sq)
                for sl in slabs:
                    tile[rows, sl] = tile[rows, sl] * inv * gout_ref[:, sl]
        row_loop(epilogue)
        o_copy(i, slot).start()

        @pl.when(i == n_tiles - 1)
        def _():
            o_copy(i, slot).wait()


def _ffn(x, g_in, wg, wu, wd, g_out=None):
    n, d = x.shape
    f = wg.shape[1]
    bm = _pick(n, (1024, 512, 256, 128, 32))
    bf = _pick(f, (256, 128))
    assert f % bf == 0 and f // bf >= 2 and bm % (2 * FFN_ROW_CHUNK) == 0
    norm_out = g_out is not None
    gain = pl.BlockSpec((1, d), lambda i, j: (0, 0))
    return pl.pallas_call(
        functools.partial(_ffn_kernel, norm_out=norm_out),
        grid=(n // bm, f // bf),
        in_specs=[pl.BlockSpec(memory_space=pl.ANY),
                  gain,
                  pl.BlockSpec((d, bf), lambda i, j: (0, j)),
                  pl.BlockSpec((d, bf), lambda i, j: (0, j)),
                  pl.BlockSpec((bf, d), lambda i, j: (j, 0)),
                  gain],
        out_specs=pl.BlockSpec(memory_space=pl.ANY),
        out_shape=jax.ShapeDtypeStruct((n, d), F32),
        scratch_shapes=[pltpu.VMEM((2, bm, d), F32), pltpu.VMEM((bm, d), BF16),
                        pltpu.SemaphoreType.DMA((2,)), pltpu.SemaphoreType.DMA((2,))],
        compiler_params=_params(("arbitrary", "arbitrary"), 60),
        name="swiglu_ffn",
    )(x, g_in.reshape(1, d), wg, wu, wd, (g_out if norm_out else g_in).reshape(1, d))


def _qkv_kernel(a_ref, w_ref, *refs, rotary, heads_per_tile):
    o_ref = refs[-1]
    acc = jnp.dot(a_ref[...], w_ref[...], preferred_element_type=F32)
    if rotary:
        c, sa, sb = (r[...] for r in refs[:3])
    for h in range(heads_per_tile):
        x = acc[:, h * HEAD_DIM:(h + 1) * HEAD_DIM]
        if rotary:
            x = (x * c + pltpu.roll(x, HEAD_DIM - ROT_DIM // 2, 1) * sa
                 + pltpu.roll(x, ROT_DIM // 2, 1) * sb)
        o_ref[h] = x.astype(o_ref.dtype)


def _qkv_proj(u, w, col0, m, rope, t, n_rope_cols, out_dtype):
    n, d = u.shape
    bm = _pick(t, (1024, 512, 256, 128))
    bn = next(p for p in (1024, 512, 256, 128)
              if m % p == 0 and n_rope_cols % p == 0 and col0 % p == 0)
    assert n % bm == 0
    tb = t // bm
    j0 = col0 // bn
    rope_tiles = n_rope_cols // bn
    rotary = rope_tiles > 0
    tab = pl.BlockSpec((None, bm, HEAD_DIM), lambda i, j: (jnp.where(j < rope_tiles, 0, 1), i % tb, 0))
    kern = functools.partial(_qkv_kernel, rotary=rotary, heads_per_tile=bn // HEAD_DIM)
    return pl.pallas_call(
        kern,
        grid=(n // bm, m // bn),
        in_specs=[pl.BlockSpec((bm, d), lambda i, j: (i, 0)),
                  pl.BlockSpec((d, bn), lambda i, j: (0, j0 + j))] + ([tab, tab, tab] if rotary else []),
        out_specs=pl.BlockSpec((bn // HEAD_DIM, bm, HEAD_DIM), lambda i, j: (j, i, 0)),
        out_shape=jax.ShapeDtypeStruct((m // HEAD_DIM, n, HEAD_DIM), out_dtype),
        compiler_params=_params(("parallel", "arbitrary"), 56),
        name="qkv_proj",
    )(u, w, *(rope if rotary else ()))


def _out_proj_kernel(a_ref, w_ref, r_ref, o_ref):
    o_ref[...] = r_ref[...] + jnp.dot(a_ref[...], w_ref[...], preferred_element_type=F32)


def _out_proj(a, w, res):
    n, k = a.shape
    m = w.shape[1]
    bm = _pick(n, (1024, 512, 256, 128, 8))
    bn = _pick(m, (1024, 512, 256, 128))
    return pl.pallas_call(
        _out_proj_kernel,
        grid=(n // bm, m // bn),
        in_specs=[pl.BlockSpec((bm, k), lambda i, j: (i, 0)),
                  pl.BlockSpec((k, bn), lambda i, j: (0, j)),
                  pl.BlockSpec((bm, bn), lambda i, j: (i, j))],
        out_specs=pl.BlockSpec((bm, bn), lambda i, j: (i, j)),
        out_shape=jax.ShapeDtypeStruct((n, m), F32),
        compiler_params=_params(("parallel", "arbitrary"), 56),
        name="out_proj",
    )(a, w, res)


def _rope_tables(t):
    pos = jnp.arange(t, dtype=F32)
    inv = ROPE_THETA ** (-jnp.arange(0, ROT_DIM, 2, dtype=F32) / ROT_DIM)
    ang = pos[:, None] * inv[None, :]
    cos, sin = jnp.cos(ang), jnp.sin(ang)
    half = ROT_DIM // 2
    z_half = jnp.zeros((t, half), F32)
    z_rest = jnp.zeros((t, HEAD_DIM - ROT_DIM), F32)
    c = jnp.concatenate([cos, cos, jnp.ones((t, HEAD_DIM - ROT_DIM), F32)], axis=1)
    sa = jnp.concatenate([-sin, z_half, z_rest], axis=1)
    sb = jnp.concatenate([z_half, sin, z_rest], axis=1)
    return (jnp.stack([c, jnp.ones_like(c)]), jnp.stack([sa, jnp.zeros_like(sa)]),
            jnp.stack([sb, jnp.zeros_like(sb)]))


DIL_UNROLL = 8


def _dilated_kernel(q_ref, k_ref, v_ref, o_ref, m_scr, l_scr, mask_scr, *, t, configs):
    scale = HEAD_DIM ** -0.5
    win = 2 * QBLK
    qi = lax.broadcasted_iota(jnp.int32, (QBLK, win), 0)
    ki = lax.broadcasted_iota(jnp.int32, (QBLK, win), 1)

    halves = sorted({half for half, _ in configs})
    for hi, half in enumerate(halves):
        for vi, off in enumerate((0, half, QBLK)):
            mask_scr[3 * hi + vi] = jnp.where(jnp.abs(qi - ki + off) <= half, 0.0, NEG_INF)

    for ci, (half, d) in enumerate(configs):
        l = t // d
        nblk = l // QBLK
        first, final = ci == 0, ci == len(configs) - 1
        shift = d.bit_length() - 1
        mask0 = 3 * halves.index(half)

        def rows(r, start, size, d=d):
            if d == 1:
                return pl.ds(pl.multiple_of(start, 8), size)
            return pl.ds(r + d * start, size, stride=d)

        def scores(idx, half=half, d=d, l=l, nblk=nblk, shift=shift, rows=rows, mask0=mask0):
            i = lax.shift_right_logical(idx, shift)
            r = idx - (i << shift)
            q0 = i * QBLK
            k0 = jnp.clip(q0 - half, 0, l - win)
            rq, rk = rows(r, q0, QBLK), rows(r, k0, win)
            q = q_ref[rq, :].astype(BF16)
            k = k_ref[rk, :].astype(BF16)
            s = lax.dot_general(q, k, (((1,), (1,)), ((), ())), preferred_element_type=F32) * scale
            placement = jnp.where(i == 0, 0, jnp.where(i == nblk - 1, 2, 1))
            return rq, rk, s + mask_scr[mask0 + placement]

        def softmax(rq, rk, s):
            m = jnp.max(s, axis=-1, keepdims=True)
            p = jnp.exp(s - m)
            den = jnp.sum(p, axis=-1, keepdims=True)
            return rq, rk, m, den, p.astype(BF16)

        def values(rq, rk, m, den, p):
            v = v_ref[rk, :].astype(BF16)
            return rq, m, den, jnp.dot(p, v, preferred_element_type=F32)

        def merge(rq, m, den, acc, first=first, final=final):
            if first:
                o_ref[rq, :] = acc
                m_scr[rq, :] = jnp.broadcast_to(m, (QBLK, LANES))
                l_scr[rq, :] = jnp.broadcast_to(den, (QBLK, LANES))
                return
            m_run = m_scr[rq, :]
            m_new = jnp.maximum(m_run, m)
            a_run = jnp.exp(m_run - m_new)
            a_blk = jnp.exp(m - m_new)
            num = o_ref[rq, :] * a_run + acc * a_blk
            tot = l_scr[rq, :] * a_run + den * a_blk
            if final:
                o_ref[rq, :] = num / tot
            else:
                o_ref[rq, :] = num
                m_scr[rq, :] = m_new
                l_scr[rq, :] = tot

        total = d * nblk
        unroll = DIL_UNROLL if total % DIL_UNROLL == 0 else 1

        def trip(it, carry, scores=scores, softmax=softmax, values=values, merge=merge, unroll=unroll):
            parts = [scores(it * unroll + u) for u in range(unroll)]
            parts = [softmax(*part) for part in parts]
            parts = [values(*part) for part in parts]
            for part in parts:
                merge(*part)
            return carry

        lax.fori_loop(0, total // unroll, trip, 0)


def _dilated_attention(qkv_a, b, t, n_heads):
    n = b * t
    nh = n_heads
    configs = tuple(sorted(((window // (2 * dil), dil) for window, dil in DILATED_CONFIGS),
                           key=lambda c: -c[1]))
    n_masks = 3 * len({half for half, _ in configs})
    for half, d in configs:
        assert d & (d - 1) == 0 and t % (d * QBLK) == 0 and t // d >= 2 * QBLK and half <= QBLK // 2
    blk = (None, t, HEAD_DIM)
    return pl.pallas_call(
        functools.partial(_dilated_kernel, t=t, configs=configs),
        grid=(b, nh),
        in_specs=[pl.BlockSpec(blk, lambda bi, h: (h, bi, 0)),
                  pl.BlockSpec(blk, lambda bi, h: (nh + h, bi, 0)),
                  pl.BlockSpec(blk, lambda bi, h: (2 * nh + h, bi, 0))],
        out_specs=pl.BlockSpec((t, HEAD_DIM), lambda bi, h: (bi, h)),
        out_shape=jax.ShapeDtypeStruct((n, nh * HEAD_DIM), F32),
        scratch_shapes=[pltpu.VMEM((t, LANES), F32), pltpu.VMEM((t, LANES), F32),
                        pltpu.VMEM((n_masks, QBLK, 2 * QBLK), F32)],
        compiler_params=_params(("parallel", "arbitrary"), 56),
        name="dilated_attn",
    )(qkv_a, qkv_a, qkv_a)


NBR_UNROLL = 8


def _nbr_kernel(q_ref, k_ref, v_ref, bias_ref, o_ref, *, rows):
    scale = HEAD_DIM ** -0.5
    wk = WIN_ROWS * GRID_W

    def scores(r):
        rs = jnp.clip(r - WIN_ROWS // 2, 0, rows - WIN_ROWS)
        q0 = pl.multiple_of(r * GRID_W, GRID_W)
        k0 = pl.multiple_of(rs * GRID_W, GRID_W)
        q = q_ref[pl.ds(q0, GRID_W), :]
        k = k_ref[pl.ds(k0, wk), :]
        s = lax.dot_general(q, k, (((1,), (1,)), ((), ())), preferred_element_type=F32) * scale
        return q0, k0, s + bias_ref[rs - r + (WIN_ROWS - 1)]

    def softmax(q0, k0, s):
        m = jnp.max(s, axis=-1, keepdims=True)
        p = jnp.exp(s - m)
        return q0, k0, jnp.sum(p, axis=-1, keepdims=True), p.astype(BF16)

    def values(q0, k0, den, p):
        o = jnp.dot(p, v_ref[pl.ds(k0, wk), :], preferred_element_type=F32)
        o_ref[pl.ds(q0, GRID_W), :] = o / den

    unroll = NBR_UNROLL if rows % NBR_UNROLL == 0 else 1

    def trip(it, carry):
        parts = [scores(it * unroll + u) for u in range(unroll)]
        parts = [softmax(*part) for part in parts]
        for part in parts:
            values(*part)
        return carry

    lax.fori_loop(0, rows // unroll, trip, 0)


def _nbr_bias_tables(rel_bias):
    nh = rel_bias.shape[0]
    c = jnp.arange(GRID_W)
    cs = jnp.clip(c - WIN_COLS // 2, 0, GRID_W - WIN_COLS)
    col_mask = (c[None, :] >= cs[:, None]) & (c[None, :] < cs[:, None] + WIN_COLS)
    dc = jnp.clip(c[None, :] - c[:, None], -(WIN_COLS - 1), WIN_COLS - 1) + (WIN_COLS - 1)
    tab = rel_bias.astype(F32)[:, :, dc]
    tab = jnp.where(col_mask[None, None], tab, NEG_INF)
    idx = jnp.arange(WIN_ROWS)[:, None] + jnp.arange(WIN_ROWS)[None, :]
    tab = tab[:, idx]
    return tab.transpose(0, 1, 3, 2, 4).reshape(nh, WIN_ROWS, GRID_W, WIN_ROWS * GRID_W)


def _nbr_attention(qkvh, bias_tab, b, t, head0, n_heads):
    n = b * t
    rows = t // GRID_W
    assert t % GRID_W == 0 and rows >= WIN_ROWS
    nh = n_heads
    blk = (None, t, HEAD_DIM)
    kern = functools.partial(_nbr_kernel, rows=rows)
    return pl.pallas_call(
        kern,
        grid=(b, nh),
        in_specs=[pl.BlockSpec(blk, lambda bi, h: (head0 + h, bi, 0)),
                  pl.BlockSpec(blk, lambda bi, h: (head0 + nh + h, bi, 0)),
                  pl.BlockSpec(blk, lambda bi, h: (head0 + 2 * nh + h, bi, 0)),
                  pl.BlockSpec((None, WIN_ROWS, GRID_W, WIN_ROWS * GRID_W), lambda bi, h: (h, 0, 0, 0))],
        out_specs=pl.BlockSpec((t, HEAD_DIM), lambda bi, h: (bi, h)),
        out_shape=jax.ShapeDtypeStruct((n, nh * HEAD_DIM), F32),
        compiler_params=_params(("parallel", "arbitrary"), 48),
        name="nbr_attn",
    )(qkvh, qkvh, qkvh, bias_tab)


def _out_norms_kernel(oa_ref, ob_ref, ga_ref, gb_ref, out_ref):
    wa = oa_ref.shape[1]
    out_ref[:, :wa] = _rms(oa_ref[...], ga_ref[...]).astype(out_ref.dtype)
    out_ref[:, wa:] = _rms(ob_ref[...], gb_ref[...]).astype(out_ref.dtype)


def _out_norms(oa, ob, ga, gb):
    n, wa = oa.shape
    wb = ob.shape[1]
    bm = _pick(n, (256, 128, 8))
    return pl.pallas_call(
        _out_norms_kernel,
        grid=(n // bm,),
        in_specs=[pl.BlockSpec((bm, wa), lambda i: (i, 0)),
                  pl.BlockSpec((bm, wb), lambda i: (i, 0)),
                  pl.BlockSpec((1, wa), lambda i: (0, 0)),
                  pl.BlockSpec((1, wb), lambda i: (0, 0))],
        out_specs=pl.BlockSpec((bm, wa + wb), lambda i: (i, 0)),
        out_shape=jax.ShapeDtypeStruct((n, wa + wb), BF16),
        compiler_params=_params(("parallel",), 40),
        name="out_norms",
    )(oa, ob, ga.reshape(1, wa), gb.reshape(1, wb))


def _layer(x, b, t, wts, final_gain):
    (ffn1_norm, ffn1, mix_norm, w_in, bias_tab, out_norm_a, out_norm_b, w_out, ffn2_norm, ffn2) = wts
    wa = out_norm_a.shape[0]
    wb = out_norm_b.shape[0]
    nha, nhb = wa // HEAD_DIM, wb // HEAD_DIM
    rope = _rope_tables(t)

    h1 = _ffn(x, ffn1_norm, *ffn1)
    u = _rmsnorm(h1, mix_norm, BF16)

    qkv_a = _qkv_proj(u, w_in, 0, 3 * wa, rope, t, 2 * wa, F32)
    qkv_b = _qkv_proj(u, w_in, 3 * wa, 3 * wb, rope, t, 0, BF16)
    oa = _dilated_attention(qkv_a, b, t, nha)
    ob = _nbr_attention(qkv_b, bias_tab, b, t, 0, nhb)
    merged = _out_norms(oa, ob, out_norm_a, out_norm_b)
    h2 = _out_proj(merged, w_out, h1)

    return _ffn(h2, ffn2_norm, *ffn2, g_out=final_gain)


def kernel(x_prompt, x_sample, ffn1_norm, ffn1_w_gate, ffn1_w_up, ffn1_w_down, mix_norm, w_in, nbr_rel_bias, out_norm_a, out_norm_b, w_out, ffn2_norm, ffn2_w_gate, ffn2_w_up, ffn2_w_down, final_norm):
    depth = ffn1_norm.shape[0]
    layers = []
    for i in range(depth):
        layers.append((ffn1_norm[i],
                       tuple(_cast_bf16(w, i) for w in (ffn1_w_gate, ffn1_w_up, ffn1_w_down)),
                       mix_norm[i], _cast_bf16(w_in, i), _nbr_bias_tables(nbr_rel_bias[i]),
                       out_norm_a[i], out_norm_b[i], _cast_bf16(w_out, i), ffn2_norm[i],
                       tuple(_cast_bf16(w, i) for w in (ffn2_w_gate, ffn2_w_up, ffn2_w_down))))

    def run(x3):
        b, t, d = x3.shape
        h = x3.reshape(b * t, d)
        for i in range(depth):
            h = _layer(h, b, t, layers[i], final_norm if i == depth - 1 else None)
        return h.reshape(b, t, d)

    return run(x_prompt), run(x_sample)
```

```python
import functools
import math

import jax
import jax.numpy as jnp
from jax import lax
from jax.experimental import pallas as pl
from jax.experimental.pallas import tpu as pltpu

HEAD_DIM = 128
DILATED_CONFIGS = ((128, 1), (512, 4), (2048, 16))
QBLK = 128
ROPE_THETA = 500000.0
ROT_DIM = HEAD_DIM // 4
GRID_W = 64
WIN_ROWS = 8
WIN_COLS = 16
NORM_EPS = 1e-6
NEG_INF = -1e30
LOG2E = math.log2(math.e)
LANES = 128
V7X_VMEM_BYTES = 64 * 1024 * 1024

F32 = jnp.float32
BF16 = jnp.bfloat16


def _params(sem, vmem_mb):
    return pltpu.CompilerParams(dimension_semantics=sem,
                                vmem_limit_bytes=min(vmem_mb * 1024 * 1024, V7X_VMEM_BYTES - (2 << 20)))


def _pick(n, prefs):
    for p in prefs:
        if n % p == 0:
            return p
    return n


def _cast_kernel(w_ref, o_ref):
    o_ref[...] = w_ref[...].astype(o_ref.dtype)


def _cast_bf16(w, layer):
    _, r, c = w.shape
    br = _pick(r, (256, 128, 16))
    return pl.pallas_call(
        _cast_kernel,
        grid=(r // br,),
        in_specs=[pl.BlockSpec((None, br, c), lambda i: (layer, i, 0))],
        out_specs=pl.BlockSpec((br, c), lambda i: (i, 0)),
        out_shape=jax.ShapeDtypeStruct((r, c), BF16),
        compiler_params=_params(("parallel",), 56),
        name="cast_bf16",
    )(w)


def _rms(x, g):
    ms = jnp.mean(x * x, axis=-1, keepdims=True)
    return x * lax.rsqrt(ms + NORM_EPS) * g


FFN_ROW_CHUNK = 32


def _ffn_kernel(x_hbm, gin_ref, wg_ref, wu_ref, wd_ref, gout_ref, o_hbm, *rest, norm_main, normed_copy):
    if normed_copy:
        u_hbm, acc, xn_scr, in_sem, out_sem, u_sem = rest
    else:
        acc, xn_scr, in_sem, out_sem = rest
    i, j = pl.program_id(0), pl.program_id(1)
    n_tiles, n_chunks = pl.num_programs(0), pl.num_programs(1)
    _, n_rows, width = acc.shape
    slabs = [slice(c, c + LANES) for c in range(0, width, LANES)]
    slot = lax.rem(i, 2)
    tile = acc.at[slot]

    def x_copy(t, s):
        return pltpu.make_async_copy(x_hbm.at[pl.ds(t * n_rows, n_rows), :], acc.at[s], in_sem.at[s])

    def o_copy(t, s):
        return pltpu.make_async_copy(acc.at[s], o_hbm.at[pl.ds(t * n_rows, n_rows), :], out_sem.at[s])

    def u_copy(t):
        return pltpu.make_async_copy(xn_scr, u_hbm.at[pl.ds(t * n_rows, n_rows), :], u_sem.at[0])

    def row_loop(fn):
        def trip(c, carry):
            fn(pl.ds(pl.multiple_of(c * FFN_ROW_CHUNK, FFN_ROW_CHUNK), FFN_ROW_CHUNK))
            return carry
        lax.fori_loop(0, n_rows // FFN_ROW_CHUNK, trip, 0, unroll=2)

    def inv_rms(sq_sum):
        return lax.rsqrt(jnp.sum(sq_sum, axis=-1, keepdims=True) / width + NORM_EPS)

    @pl.when(j == 0)
    def _():
        @pl.when(i == 0)
        def _():
            x_copy(0, 0).start()
        x_copy(i, slot).wait()
        if normed_copy:
            @pl.when(i >= 1)
            def _():
                u_copy(i - 1).wait()

        def prologue(rows):
            sq = jnp.zeros((FFN_ROW_CHUNK, LANES), F32)
            for sl in slabs:
                v = tile[rows, sl]
                sq = sq + v * v
            inv = inv_rms(sq)
            for sl in slabs:
                v = tile[rows, sl]
                xn_scr[rows, sl] = (v * inv * gin_ref[:, sl]).astype(BF16)
                tile[rows, sl] = v + v
        row_loop(prologue)

    xn = xn_scr[...]
    g = jnp.dot(xn, wg_ref[...], preferred_element_type=F32)
    u = jnp.dot(xn, wu_ref[...], preferred_element_type=F32)
    a = (g * jax.nn.sigmoid(g) * u).astype(BF16)
    tile[...] += jnp.dot(a, wd_ref[...], preferred_element_type=F32)

    @pl.when(j == 1)
    def _():
        @pl.when(i >= 1)
        def _():
            o_copy(i - 1, 1 - slot).wait()

        @pl.when(i + 1 < n_tiles)
        def _():
            x_copy(i + 1, 1 - slot).start()

    @pl.when(j == n_chunks - 1)
    def _():
        def epilogue(rows):
            sq = jnp.zeros((FFN_ROW_CHUNK, LANES), F32)
            for sl in slabs:
                h = 0.5 * tile[rows, sl]
                tile[rows, sl] = h
                if norm_main or normed_copy:
                    sq = sq + h * h
            if norm_main or normed_copy:
                inv = inv_rms(sq)
                for sl in slabs:
                    y = tile[rows, sl] * inv * gout_ref[:, sl]
                    if norm_main:
                        tile[rows, sl] = y
                    if normed_copy:
                        xn_scr[rows, sl] = y.astype(BF16)
        row_loop(epilogue)
        o_copy(i, slot).start()
        if normed_copy:
            u_copy(i).start()

        @pl.when(i == n_tiles - 1)
        def _():
            o_copy(i, slot).wait()
            if normed_copy:
                u_copy(i).wait()


def _ffn(x, g_in, wg, wu, wd, g_out, norm_main=False, normed_copy=False):
    n, d = x.shape
    f = wg.shape[1]
    bm = _pick(n, (1024, 512, 256, 128, 32))
    bf = _pick(f, (256, 128))
    assert f % bf == 0 and f // bf >= 2 and bm % (2 * FFN_ROW_CHUNK) == 0
    gain = pl.BlockSpec((1, d), lambda i, j: (0, 0))
    hbm = pl.BlockSpec(memory_space=pl.ANY)
    main = jax.ShapeDtypeStruct((n, d), F32)
    dma2 = pltpu.SemaphoreType.DMA((2,))
    return pl.pallas_call(
        functools.partial(_ffn_kernel, norm_main=norm_main, normed_copy=normed_copy),
        grid=(n // bm, f // bf),
        in_specs=[hbm,
                  gain,
                  pl.BlockSpec((d, bf), lambda i, j: (0, j)),
                  pl.BlockSpec((d, bf), lambda i, j: (0, j)),
                  pl.BlockSpec((bf, d), lambda i, j: (j, 0)),
                  gain],
        out_specs=[hbm, hbm] if normed_copy else hbm,
        out_shape=[main, jax.ShapeDtypeStruct((n, d), BF16)] if normed_copy else main,
        scratch_shapes=[pltpu.VMEM((2, bm, d), F32), pltpu.VMEM((bm, d), BF16), dma2, dma2]
        + ([pltpu.SemaphoreType.DMA((1,))] if normed_copy else []),
        compiler_params=_params(("arbitrary", "arbitrary"), 60),
        name="swiglu_ffn",
    )(x, g_in.reshape(1, d), wg, wu, wd, g_out.reshape(1, d))


def _qkv_kernel(a_ref, w_ref, *refs, rotary, heads_per_tile):
    o_ref = refs[-1]
    acc = jnp.dot(a_ref[...], w_ref[...], preferred_element_type=F32)
    if rotary:
        c, sa, sb = (r[...] for r in refs[:3])
    for h in range(heads_per_tile):
        x = acc[:, h * HEAD_DIM:(h + 1) * HEAD_DIM]
        if rotary:
            x = (x * c + pltpu.roll(x, HEAD_DIM - ROT_DIM // 2, 1) * sa
                 + pltpu.roll(x, ROT_DIM // 2, 1) * sb)
        o_ref[h] = x.astype(o_ref.dtype)


def _qkv_proj(u, w, col0, m, rope, t, n_rope_cols, out_dtype):
    n, d = u.shape
    bm = _pick(t, (1024, 512, 256, 128))
    bn = next(p for p in (1024, 512, 256, 128)
              if m % p == 0 and n_rope_cols % p == 0 and col0 % p == 0)
    assert n % bm == 0
    tb = t // bm
    j0 = col0 // bn
    rope_tiles = n_rope_cols // bn
    rotary = rope_tiles > 0
    tab = pl.BlockSpec((None, bm, HEAD_DIM), lambda i, j: (jnp.where(j < rope_tiles, 0, 1), i % tb, 0))
    kern = functools.partial(_qkv_kernel, rotary=rotary, heads_per_tile=bn // HEAD_DIM)
    return pl.pallas_call(
        kern,
        grid=(n // bm, m // bn),
        in_specs=[pl.BlockSpec((bm, d), lambda i, j: (i, 0)),
                  pl.BlockSpec((d, bn), lambda i, j: (0, j0 + j))] + ([tab, tab, tab] if rotary else []),
        out_specs=pl.BlockSpec((bn // HEAD_DIM, bm, HEAD_DIM), lambda i, j: (j, i, 0)),
        out_shape=jax.ShapeDtypeStruct((m // HEAD_DIM, n, HEAD_DIM), out_dtype),
        compiler_params=_params(("parallel", "arbitrary"), 56),
        name="qkv_proj",
    )(u, w, *(rope if rotary else ()))


def _out_proj_kernel(a_ref, w_ref, r_ref, o_ref):
    o_ref[...] = r_ref[...] + jnp.dot(a_ref[...], w_ref[...], preferred_element_type=F32)


def _out_proj(a, w, res):
    n, k = a.shape
    m = w.shape[1]
    bm = _pick(n, (1024, 512, 256, 128, 8))
    bn = _pick(m, (1024, 512, 256, 128))
    return pl.pallas_call(
        _out_proj_kernel,
        grid=(n // bm, m // bn),
        in_specs=[pl.BlockSpec((bm, k), lambda i, j: (i, 0)),
                  pl.BlockSpec((k, bn), lambda i, j: (0, j)),
                  pl.BlockSpec((bm, bn), lambda i, j: (i, j))],
        out_specs=pl.BlockSpec((bm, bn), lambda i, j: (i, j)),
        out_shape=jax.ShapeDtypeStruct((n, m), F32),
        compiler_params=_params(("parallel", "arbitrary"), 56),
        name="out_proj",
    )(a, w, res)


def _rope_tables(t):
    pos = jnp.arange(t, dtype=F32)
    inv = ROPE_THETA ** (-jnp.arange(0, ROT_DIM, 2, dtype=F32) / ROT_DIM)
    ang = pos[:, None] * inv[None, :]
    cos, sin = jnp.cos(ang), jnp.sin(ang)
    half = ROT_DIM // 2
    z_half = jnp.zeros((t, half), F32)
    z_rest = jnp.zeros((t, HEAD_DIM - ROT_DIM), F32)
    c = jnp.concatenate([cos, cos, jnp.ones((t, HEAD_DIM - ROT_DIM), F32)], axis=1)
    sa = jnp.concatenate([-sin, z_half, z_rest], axis=1)
    sb = jnp.concatenate([z_half, sin, z_rest], axis=1)
    return (jnp.stack([c, jnp.ones_like(c)]), jnp.stack([sa, jnp.zeros_like(sa)]),
            jnp.stack([sb, jnp.zeros_like(sb)]))


DIL_UNROLL = 8


def _dilated_kernel(q_ref, k_ref, v_ref, o_ref, m_scr, l_scr, mask_scr, *, t, configs):
    scale = HEAD_DIM ** -0.5 * LOG2E
    win = 2 * QBLK
    qi = lax.broadcasted_iota(jnp.int32, (QBLK, win), 0)
    ki = lax.broadcasted_iota(jnp.int32, (QBLK, win), 1)

    halves = sorted({half for half, _ in configs})
    for hi, half in enumerate(halves):
        for vi, off in enumerate((0, half, QBLK)):
            mask_scr[3 * hi + vi] = jnp.where(jnp.abs(qi - ki + off) <= half, 0.0, NEG_INF)

    for ci, (half, d) in enumerate(configs):
        l = t // d
        nblk = l // QBLK
        first, final = ci == 0, ci == len(configs) - 1
        shift = d.bit_length() - 1
        mask0 = 3 * halves.index(half)

        def rows(r, start, size, d=d):
            if d == 1:
                return pl.ds(pl.multiple_of(start, 8), size)
            return pl.ds(r + d * start, size, stride=d)

        def scores(idx, half=half, d=d, l=l, nblk=nblk, shift=shift, rows=rows, mask0=mask0):
            i = lax.shift_right_logical(idx, shift)
            r = idx - (i << shift)
            q0 = i * QBLK
            k0 = jnp.clip(q0 - half, 0, l - win)
            rq, rk = rows(r, q0, QBLK), rows(r, k0, win)
            q = q_ref[rq, :].astype(BF16)
            k = k_ref[rk, :].astype(BF16)
            s = lax.dot_general(q, k, (((1,), (1,)), ((), ())), preferred_element_type=F32) * scale
            placement = jnp.where(i == 0, 0, jnp.where(i == nblk - 1, 2, 1))
            return rq, rk, s + mask_scr[mask0 + placement]

        def softmax(rq, rk, s):
            m = jnp.max(s, axis=-1, keepdims=True)
            p = jnp.exp2(s - m)
            den = jnp.sum(p, axis=-1, keepdims=True)
            return rq, rk, m, den, p.astype(BF16)

        def values(rq, rk, m, den, p):
            v = v_ref[rk, :].astype(BF16)
            return rq, m, den, jnp.dot(p, v, preferred_element_type=F32)

        def merge(rq, m, den, acc, first=first, final=final):
            if first:
                o_ref[rq, :] = acc
                m_scr[rq, :] = jnp.broadcast_to(m, (QBLK, LANES))
                l_scr[rq, :] = jnp.broadcast_to(den, (QBLK, LANES))
                return
            m_run = m_scr[rq, :]
            m_new = jnp.maximum(m_run, m)
            a_run = jnp.exp2(m_run - m_new)
            a_blk = jnp.exp2(m - m_new)
            num = o_ref[rq, :] * a_run + acc * a_blk
            tot = l_scr[rq, :] * a_run + den * a_blk
            if final:
                o_ref[rq, :] = num / tot
            else:
                o_ref[rq, :] = num
                m_scr[rq, :] = m_new
                l_scr[rq, :] = tot

        total = d * nblk
        unroll = DIL_UNROLL if total % DIL_UNROLL == 0 else 1

        def trip(it, carry, scores=scores, softmax=softmax, values=values, merge=merge, unroll=unroll):
            parts = [scores(it * unroll + u) for u in range(unroll)]
            parts = [softmax(*part) for part in parts]
            parts = [values(*part) for part in parts]
            for part in parts:
                merge(*part)
            return carry

        lax.fori_loop(0, total // unroll, trip, 0)


def _dilated_attention(qkv_a, b, t, n_heads):
    n = b * t
    nh = n_heads
    configs = tuple(sorted(((window // (2 * dil), dil) for window, dil in DILATED_CONFIGS),
                           key=lambda c: -c[1]))
    n_masks = 3 * len({half for half, _ in configs})
    for half, d in configs:
        assert d & (d - 1) == 0 and t % (d * QBLK) == 0 and t // d >= 2 * QBLK and half <= QBLK // 2
    blk = (None, t, HEAD_DIM)
    return pl.pallas_call(
        functools.partial(_dilated_kernel, t=t, configs=configs),
        grid=(b, nh),
        in_specs=[pl.BlockSpec(blk, lambda bi, h: (h, bi, 0)),
                  pl.BlockSpec(blk, lambda bi, h: (nh + h, bi, 0)),
                  pl.BlockSpec(blk, lambda bi, h: (2 * nh + h, bi, 0))],
        out_specs=pl.BlockSpec((t, HEAD_DIM), lambda bi, h: (bi, h)),
        out_shape=jax.ShapeDtypeStruct((n, nh * HEAD_DIM), F32),
        scratch_shapes=[pltpu.VMEM((t, LANES), F32), pltpu.VMEM((t, LANES), F32),
                        pltpu.VMEM((n_masks, QBLK, 2 * QBLK), F32)],
        compiler_params=_params(("parallel", "arbitrary"), 56),
        name="dilated_attn",
    )(qkv_a, qkv_a, qkv_a)


NBR_UNROLL = 8


def _nbr_kernel(q_ref, k_ref, v_ref, bias_ref, o_ref, *, rows):
    scale = HEAD_DIM ** -0.5 * LOG2E
    wk = WIN_ROWS * GRID_W

    def scores(r):
        rs = jnp.clip(r - WIN_ROWS // 2, 0, rows - WIN_ROWS)
        q0 = pl.multiple_of(r * GRID_W, GRID_W)
        k0 = pl.multiple_of(rs * GRID_W, GRID_W)
        q = q_ref[pl.ds(q0, GRID_W), :]
        k = k_ref[pl.ds(k0, wk), :]
        s = lax.dot_general(q, k, (((1,), (1,)), ((), ())), preferred_element_type=F32) * scale
        return q0, k0, s + bias_ref[rs - r + (WIN_ROWS - 1)]

    def softmax(q0, k0, s):
        m = jnp.max(s, axis=-1, keepdims=True)
        p = jnp.exp2(s - m)
        return q0, k0, jnp.sum(p, axis=-1, keepdims=True), p.astype(BF16)

    def values(q0, k0, den, p):
        o = jnp.dot(p, v_ref[pl.ds(k0, wk), :], preferred_element_type=F32)
        o_ref[pl.ds(q0, GRID_W), :] = o / den

    unroll = NBR_UNROLL if rows % NBR_UNROLL == 0 else 1

    def trip(it, carry):
        parts = [scores(it * unroll + u) for u in range(unroll)]
        parts = [softmax(*part) for part in parts]
        for part in parts:
            values(*part)
        return carry

    lax.fori_loop(0, rows // unroll, trip, 0)


def _nbr_bias_tables(rel_bias):
    nh = rel_bias.shape[0]
    c = jnp.arange(GRID_W)
    cs = jnp.clip(c - WIN_COLS // 2, 0, GRID_W - WIN_COLS)
    col_mask = (c[None, :] >= cs[:, None]) & (c[None, :] < cs[:, None] + WIN_COLS)
    dc = jnp.clip(c[None, :] - c[:, None], -(WIN_COLS - 1), WIN_COLS - 1) + (WIN_COLS - 1)
    tab = rel_bias.astype(F32)[:, :, dc]
    tab = jnp.where(col_mask[None, None], tab * LOG2E, NEG_INF)
    idx = jnp.arange(WIN_ROWS)[:, None] + jnp.arange(WIN_ROWS)[None, :]
    tab = tab[:, idx]
    return tab.transpose(0, 1, 3, 2, 4).reshape(nh, WIN_ROWS, GRID_W, WIN_ROWS * GRID_W)


def _nbr_attention(qkvh, bias_tab, b, t, head0, n_heads):
    n = b * t
    rows = t // GRID_W
    assert t % GRID_W == 0 and rows >= WIN_ROWS
    nh = n_heads
    blk = (None, t, HEAD_DIM)
    kern = functools.partial(_nbr_kernel, rows=rows)
    return pl.pallas_call(
        kern,
        grid=(b, nh),
        in_specs=[pl.BlockSpec(blk, lambda bi, h: (head0 + h, bi, 0)),
                  pl.BlockSpec(blk, lambda bi, h: (head0 + nh + h, bi, 0)),
                  pl.BlockSpec(blk, lambda bi, h: (head0 + 2 * nh + h, bi, 0)),
                  pl.BlockSpec((None, WIN_ROWS, GRID_W, WIN_ROWS * GRID_W), lambda bi, h: (h, 0, 0, 0))],
        out_specs=pl.BlockSpec((t, HEAD_DIM), lambda bi, h: (bi, h)),
        out_shape=jax.ShapeDtypeStruct((n, nh * HEAD_DIM), F32),
        compiler_params=_params(("parallel", "arbitrary"), 48),
        name="nbr_attn",
    )(qkvh, qkvh, qkvh, bias_tab)


def _out_norms_kernel(oa_ref, ob_ref, ga_ref, gb_ref, out_ref):
    wa = oa_ref.shape[1]
    out_ref[:, :wa] = _rms(oa_ref[...], ga_ref[...]).astype(out_ref.dtype)
    out_ref[:, wa:] = _rms(ob_ref[...], gb_ref[...]).astype(out_ref.dtype)


def _out_norms(oa, ob, ga, gb):
    n, wa = oa.shape
    wb = ob.shape[1]
    bm = _pick(n, (256, 128, 8))
    return pl.pallas_call(
        _out_norms_kernel,
        grid=(n // bm,),
        in_specs=[pl.BlockSpec((bm, wa), lambda i: (i, 0)),
                  pl.BlockSpec((bm, wb), lambda i: (i, 0)),
                  pl.BlockSpec((1, wa), lambda i: (0, 0)),
                  pl.BlockSpec((1, wb), lambda i: (0, 0))],
        out_specs=pl.BlockSpec((bm, wa + wb), lambda i: (i, 0)),
        out_shape=jax.ShapeDtypeStruct((n, wa + wb), BF16),
        compiler_params=_params(("parallel",), 40),
        name="out_norms",
    )(oa, ob, ga.reshape(1, wa), gb.reshape(1, wb))


def _layer(x, b, t, wts, final_gain):
    (ffn1_norm, ffn1, mix_norm, w_in, bias_tab, out_norm_a, out_norm_b, w_out, ffn2_norm, ffn2) = wts
    wa = out_norm_a.shape[0]
    wb = out_norm_b.shape[0]
    nha, nhb = wa // HEAD_DIM, wb // HEAD_DIM
    rope = _rope_tables(t)

    h1, u = _ffn(x, ffn1_norm, *ffn1, mix_norm, normed_copy=True)

    qkv_a = _qkv_proj(u, w_in, 0, 3 * wa, rope, t, 2 * wa, F32)
    qkv_b = _qkv_proj(u, w_in, 3 * wa, 3 * wb, rope, t, 0, BF16)
    oa = _dilated_attention(qkv_a, b, t, nha)
    ob = _nbr_attention(qkv_b, bias_tab, b, t, 0, nhb)
    merged = _out_norms(oa, ob, out_norm_a, out_norm_b)
    h2 = _out_proj(merged, w_out, h1)

    if final_gain is None:
        return _ffn(h2, ffn2_norm, *ffn2, ffn2_norm)
    return _ffn(h2, ffn2_norm, *ffn2, final_gain, norm_main=True)


def kernel(x_prompt, x_sample, ffn1_norm, ffn1_w_gate, ffn1_w_up, ffn1_w_down, mix_norm, w_in, nbr_rel_bias, out_norm_a, out_norm_b, w_out, ffn2_norm, ffn2_w_gate, ffn2_w_up, ffn2_w_down, final_norm):
    depth = ffn1_norm.shape[0]
    layers = []
    for i in range(depth):
        layers.append((ffn1_norm[i],
                       tuple(_cast_bf16(w, i) for w in (ffn1_w_gate, ffn1_w_up, ffn1_w_down)),
                       mix_norm[i], _cast_bf16(w_in, i), _nbr_bias_tables(nbr_rel_bias[i]),
                       out_norm_a[i], out_norm_b[i], _cast_bf16(w_out, i), ffn2_norm[i],
                       tuple(_cast_bf16(w, i) for w in (ffn2_w_gate, ffn2_w_up, ffn2_w_down))))

    def run(x3):
        b, t, d = x3.shape
        h = x3.reshape(b * t, d)
        for i in range(depth):
            h = _layer(h, b, t, layers[i], final_norm if i == depth - 1 else None)
        return h.reshape(b, t, d)

    return run(x_prompt), run(x_sample)
```

```python
import functools
import math

import jax
import jax.numpy as jnp
from jax import lax
from jax.experimental import pallas as pl
from jax.experimental.pallas import tpu as pltpu

HEAD_DIM = 128
DILATED_CONFIGS = ((128, 1), (512, 4), (2048, 16))
QBLK = 128
ROPE_THETA = 500000.0
ROT_DIM = HEAD_DIM // 4
GRID_W = 64
WIN_ROWS = 8
WIN_COLS = 16
NORM_EPS = 1e-6
NEG_INF = -1e30
LOG2E = math.log2(math.e)
LANES = 128
V7X_VMEM_BYTES = 64 * 1024 * 1024

F32 = jnp.float32
BF16 = jnp.bfloat16


def _params(sem, vmem_mb):
    return pltpu.CompilerParams(dimension_semantics=sem,
                                vmem_limit_bytes=min(vmem_mb * 1024 * 1024, V7X_VMEM_BYTES - (2 << 20)))


def _pick(n, prefs):
    for p in prefs:
        if n % p == 0:
            return p
    return n


def _cast_kernel(w_ref, o_ref):
    o_ref[...] = w_ref[...].astype(o_ref.dtype)


def _cast_bf16(w, layer):
    _, r, c = w.shape
    br = _pick(r, (256, 128, 16))
    return pl.pallas_call(
        _cast_kernel,
        grid=(r // br,),
        in_specs=[pl.BlockSpec((None, br, c), lambda i: (layer, i, 0))],
        out_specs=pl.BlockSpec((br, c), lambda i: (i, 0)),
        out_shape=jax.ShapeDtypeStruct((r, c), BF16),
        compiler_params=_params(("parallel",), 56),
        name="cast_bf16",
    )(w)


def _rms(x, g):
    ms = jnp.mean(x * x, axis=-1, keepdims=True)
    return x * lax.rsqrt(ms + NORM_EPS) * g


def _rmsnorm_kernel(x_ref, g_ref, o_ref):
    o_ref[...] = _rms(x_ref[...], g_ref[...]).astype(o_ref.dtype)


def _rmsnorm(x, g, out_dtype):
    n, d = x.shape
    bm = _pick(n, (256, 128, 8))
    return pl.pallas_call(
        _rmsnorm_kernel,
        grid=(n // bm,),
        in_specs=[pl.BlockSpec((bm, d), lambda i: (i, 0)),
                  pl.BlockSpec((1, d), lambda i: (0, 0))],
        out_specs=pl.BlockSpec((bm, d), lambda i: (i, 0)),
        out_shape=jax.ShapeDtypeStruct((n, d), out_dtype),
        compiler_params=_params(("parallel",), 40),
        name="rmsnorm",
    )(x, g.reshape(1, d))


FFN_ROW_CHUNK = 32


def _ffn_kernel(x_hbm, gin_ref, wg_ref, wu_ref, wd_ref, gout_ref, o_hbm, acc, xn_scr, in_sem, out_sem,
                *, norm_out):
    i, j = pl.program_id(0), pl.program_id(1)
    n_tiles, n_chunks = pl.num_programs(0), pl.num_programs(1)
    _, n_rows, width = acc.shape
    slabs = [slice(c, c + LANES) for c in range(0, width, LANES)]
    slot = lax.rem(i, 2)
    tile = acc.at[slot]

    def x_copy(t, s):
        return pltpu.make_async_copy(x_hbm.at[pl.ds(t * n_rows, n_rows), :], acc.at[s], in_sem.at[s])

    def o_copy(t, s):
        return pltpu.make_async_copy(acc.at[s], o_hbm.at[pl.ds(t * n_rows, n_rows), :], out_sem.at[s])

    def row_loop(fn):
        def trip(c, carry):
            fn(pl.ds(pl.multiple_of(c * FFN_ROW_CHUNK, FFN_ROW_CHUNK), FFN_ROW_CHUNK))
            return carry
        lax.fori_loop(0, n_rows // FFN_ROW_CHUNK, trip, 0, unroll=2)

    def inv_rms(sq_sum):
        return lax.rsqrt(jnp.sum(sq_sum, axis=-1, keepdims=True) / width + NORM_EPS)

    @pl.when(j == 0)
    def _():
        @pl.when(i == 0)
        def _():
            x_copy(0, 0).start()
        x_copy(i, slot).wait()

        def prologue(rows):
            sq = jnp.zeros((FFN_ROW_CHUNK, LANES), F32)
            for sl in slabs:
                v = tile[rows, sl]
                sq = sq + v * v
            inv = inv_rms(sq)
            for sl in slabs:
                v = tile[rows, sl]
                xn_scr[rows, sl] = (v * inv * gin_ref[:, sl]).astype(BF16)
                tile[rows, sl] = v + v
        row_loop(prologue)

    xn = xn_scr[...]
    g = jnp.dot(xn, wg_ref[...], preferred_element_type=F32)
    u = jnp.dot(xn, wu_ref[...], preferred_element_type=F32)
    a = (g * jax.nn.sigmoid(g) * u).astype(BF16)
    tile[...] += jnp.dot(a, wd_ref[...], preferred_element_type=F32)

    @pl.when(j == 1)
    def _():
        @pl.when(i >= 1)
        def _():
            o_copy(i - 1, 1 - slot).wait()

        @pl.when(i + 1 < n_tiles)
        def _():
            x_copy(i + 1, 1 - slot).start()

    @pl.when(j == n_chunks - 1)
    def _():
        def epilogue(rows):
            sq = jnp.zeros((FFN_ROW_CHUNK, LANES), F32)
            for sl in slabs:
                h = 0.5 * tile[rows, sl]
                tile[rows, sl] = h
                if norm_out:
                    sq = sq + h * h
            if norm_out:
                inv = inv_rms(sq)
                for sl in slabs:
                    tile[rows, sl] = tile[rows, sl] * inv * gout_ref[:, sl]
        row_loop(epilogue)
        o_copy(i, slot).start()

        @pl.when(i == n_tiles - 1)
        def _():
            o_copy(i, slot).wait()


def _ffn(x, g_in, wg, wu, wd, g_out=None):
    n, d = x.shape
    f = wg.shape[1]
    bm = _pick(n, (1024, 512, 256, 128, 32))
    bf = _pick(f, (256, 128))
    assert f % bf == 0 and f // bf >= 2 and bm % (2 * FFN_ROW_CHUNK) == 0
    norm_out = g_out is not None
    gain = pl.BlockSpec((1, d), lambda i, j: (0, 0))
    hbm = pl.BlockSpec(memory_space=pl.ANY)
    dma2 = pltpu.SemaphoreType.DMA((2,))
    return pl.pallas_call(
        functools.partial(_ffn_kernel, norm_out=norm_out),
        grid=(n // bm, f // bf),
        in_specs=[hbm,
                  gain,
                  pl.BlockSpec((d, bf), lambda i, j: (0, j)),
                  pl.BlockSpec((d, bf), lambda i, j: (0, j)),
                  pl.BlockSpec((bf, d), lambda i, j: (j, 0)),
                  gain],
        out_specs=hbm,
        out_shape=jax.ShapeDtypeStruct((n, d), F32),
        scratch_shapes=[pltpu.VMEM((2, bm, d), F32), pltpu.VMEM((bm, d), BF16), dma2, dma2],
        compiler_params=_params(("arbitrary", "arbitrary"), 60),
        name="swiglu_ffn",
    )(x, g_in.reshape(1, d), wg, wu, wd, (g_out if norm_out else g_in).reshape(1, d))


def _qkv_kernel(a_ref, w_ref, *refs, rotary, heads_per_tile):
    o_ref = refs[-1]
    acc = jnp.dot(a_ref[...], w_ref[...], preferred_element_type=F32)
    if rotary:
        c, sa, sb = (r[...] for r in refs[:3])
    for h in range(heads_per_tile):
        x = acc[:, h * HEAD_DIM:(h + 1) * HEAD_DIM]
        if rotary:
            x = (x * c + pltpu.roll(x, HEAD_DIM - ROT_DIM // 2, 1) * sa
                 + pltpu.roll(x, ROT_DIM // 2, 1) * sb)
        o_ref[h] = x.astype(o_ref.dtype)


def _qkv_proj(u, w, col0, m, rope, t, n_rope_cols, out_dtype):
    n, d = u.shape
    bm = _pick(t, (1024, 512, 256, 128))
    bn = next(p for p in (1024, 512, 256, 128)
              if m % p == 0 and n_rope_cols % p == 0 and col0 % p == 0)
    assert n % bm == 0
    tb = t // bm
    j0 = col0 // bn
    rope_tiles = n_rope_cols // bn
    rotary = rope_tiles > 0
    tab = pl.BlockSpec((None, bm, HEAD_DIM), lambda i, j: (jnp.where(j < rope_tiles, 0, 1), i % tb, 0))
    kern = functools.partial(_qkv_kernel, rotary=rotary, heads_per_tile=bn // HEAD_DIM)
    return pl.pallas_call(
        kern,
        grid=(n // bm, m // bn),
        in_specs=[pl.BlockSpec((bm, d), lambda i, j: (i, 0)),
                  pl.BlockSpec((d, bn), lambda i, j: (0, j0 + j))] + ([tab, tab, tab] if rotary else []),
        out_specs=pl.BlockSpec((bn // HEAD_DIM, bm, HEAD_DIM), lambda i, j: (j, i, 0)),
        out_shape=jax.ShapeDtypeStruct((m // HEAD_DIM, n, HEAD_DIM), out_dtype),
        compiler_params=_params(("parallel", "arbitrary"), 56),
        name="qkv_proj",
    )(u, w, *(rope if rotary else ()))


def _out_proj_kernel(a_ref, w_ref, r_ref, o_ref):
    o_ref[...] = r_ref[...] + jnp.dot(a_ref[...], w_ref[...], preferred_element_type=F32)


def _out_proj(a, w, res):
    n, k = a.shape
    m = w.shape[1]
    bm = _pick(n, (1024, 512, 256, 128, 8))
    bn = _pick(m, (1024, 512, 256, 128))
    return pl.pallas_call(
        _out_proj_kernel,
        grid=(n // bm, m // bn),
        in_specs=[pl.BlockSpec((bm, k), lambda i, j: (i, 0)),
                  pl.BlockSpec((k, bn), lambda i, j: (0, j)),
                  pl.BlockSpec((bm, bn), lambda i, j: (i, j))],
        out_specs=pl.BlockSpec((bm, bn), lambda i, j: (i, j)),
        out_shape=jax.ShapeDtypeStruct((n, m), F32),
        compiler_params=_params(("parallel", "arbitrary"), 56),
        name="out_proj",
    )(a, w, res)


def _rope_tables(t):
    pos = jnp.arange(t, dtype=F32)
    inv = ROPE_THETA ** (-jnp.arange(0, ROT_DIM, 2, dtype=F32) / ROT_DIM)
    ang = pos[:, None] * inv[None, :]
    cos, sin = jnp.cos(ang), jnp.sin(ang)
    half = ROT_DIM // 2
    z_half = jnp.zeros((t, half), F32)
    z_rest = jnp.zeros((t, HEAD_DIM - ROT_DIM), F32)
    c = jnp.concatenate([cos, cos, jnp.ones((t, HEAD_DIM - ROT_DIM), F32)], axis=1)
    sa = jnp.concatenate([-sin, z_half, z_rest], axis=1)
    sb = jnp.concatenate([z_half, sin, z_rest], axis=1)
    return (jnp.stack([c, jnp.ones_like(c)]), jnp.stack([sa, jnp.zeros_like(sa)]),
            jnp.stack([sb, jnp.zeros_like(sb)]))


DIL_UNROLL = 8


def _dilated_kernel(q_ref, k_ref, v_ref, o_ref, kp, vp, accp, mp, lp, mask_scr, *, t, p, halves):
    scale = HEAD_DIM ** -0.5 * LOG2E
    win = 2 * QBLK
    lgrp = t // p
    sub = QBLK // p
    log_p, log_sub = p.bit_length() - 1, sub.bit_length() - 1
    qi = lax.broadcasted_iota(jnp.int32, (QBLK, win), 0)
    ki = lax.broadcasted_iota(jnp.int32, (QBLK, win), 1)
    qi_regrouped = ((qi & (sub - 1)) << log_p) + (qi >> log_sub)

    for ci, half in enumerate(halves):
        rows_pos = qi_regrouped if ci == 2 else qi
        for vi, off in enumerate((0, half, QBLK)):
            mask_scr[3 * ci + vi] = jnp.where(jnp.abs(rows_pos - ki + off) <= half, 0.0, NEG_INF)

    def regroup(c, carry):
        r = lax.div(c, lgrp // QBLK)
        src = pl.ds(r + p * (c * QBLK - r * lgrp), QBLK, stride=p)
        dst = pl.ds(pl.multiple_of(c * QBLK, QBLK), QBLK)
        kp[dst, :] = k_ref[src, :]
        vp[dst, :] = v_ref[src, :]
        return carry

    lax.fori_loop(0, t // QBLK, regroup, 0, unroll=2)

    def softmax(blk, s):
        m = jnp.max(s, axis=-1, keepdims=True)
        e = jnp.exp2(s - m)
        return blk, m, jnp.sum(e, axis=-1, keepdims=True), e.astype(BF16)

    def run_pass(ci, d, load_q, load_kv, load_state, store_state):
        half = halves[ci]
        l = t // d
        nblk = l // QBLK
        shift = d.bit_length() - 1

        def scores(idx):
            i = lax.shift_right_logical(idx, shift)
            r = idx - (i << shift)
            q0 = i * QBLK
            k0 = jnp.clip(q0 - half, 0, l - win)
            placement = jnp.where(i == 0, 0, jnp.where(i == nblk - 1, 2, 1))
            q = load_q(r, q0).astype(BF16)
            k = load_kv(kp, k_ref, r, k0).astype(BF16)
            s = lax.dot_general(q, k, (((1,), (1,)), ((), ())), preferred_element_type=F32) * scale
            return (r, q0, k0), s + mask_scr[3 * ci + placement]

        def values(blk, m, den, e):
            r, _, k0 = blk
            v = load_kv(vp, v_ref, r, k0).astype(BF16)
            return blk, m, den, jnp.dot(e, v, preferred_element_type=F32)

        def merge(blk, m, den, acc):
            r, q0, _ = blk
            if ci == 0:
                store_state(r, q0, acc, jnp.broadcast_to(m, (QBLK, LANES)),
                            jnp.broadcast_to(den, (QBLK, LANES)))
                return
            acc_run, m_run, l_run = load_state(r, q0)
            m_new = jnp.maximum(m_run, m)
            a_run = jnp.exp2(m_run - m_new)
            a_blk = jnp.exp2(m - m_new)
            store_state(r, q0, acc_run * a_run + acc * a_blk, m_new, l_run * a_run + den * a_blk)

        total = d * nblk
        unroll = DIL_UNROLL if total % DIL_UNROLL == 0 else 1

        def trip(it, carry):
            parts = [scores(it * unroll + u) for u in range(unroll)]
            parts = [softmax(*part) for part in parts]
            parts = [values(*part) for part in parts]
            for part in parts:
                merge(*part)
            return carry

        lax.fori_loop(0, total // unroll, trip, 0)

    def store_regrouped(rows, acc, m, den):
        accp[rows, :] = acc
        mp[rows, :] = m
        lp[rows, :] = den

    def wide_rows(r, start, size):
        rr = lax.shift_right_logical(r, log_p)
        return pl.ds((r - (rr << log_p)) * lgrp + rr + p * start, size, stride=p)

    run_pass(0, p * p,
             load_q=lambda r, q0: q_ref[pl.ds(r + p * p * q0, QBLK, stride=p * p), :],
             load_kv=lambda grouped, natural, r, k0: grouped[wide_rows(r, k0, win), :],
             load_state=None,
             store_state=lambda r, q0, *state: store_regrouped(wide_rows(r, q0, QBLK), *state))

    def mid_rows(r, start, size):
        return pl.ds(pl.multiple_of(r * lgrp + start, 8), size)

    def mid_load(r, q0):
        rows = mid_rows(r, q0, QBLK)
        return accp[rows, :], mp[rows, :], lp[rows, :]

    run_pass(1, p,
             load_q=lambda r, q0: q_ref[pl.ds(r + p * q0, QBLK, stride=p), :],
             load_kv=lambda grouped, natural, r, k0: grouped[mid_rows(r, k0, win), :],
             load_state=mid_load,
             store_state=lambda r, q0, *state: store_regrouped(mid_rows(r, q0, QBLK), *state))

    def fine_load(r, q0):
        rows = [pl.ds(pl.multiple_of(c * lgrp + (q0 >> log_p), 8), sub) for c in range(p)]
        return tuple(jnp.concatenate([ref[rc, :] for rc in rows], axis=0) for ref in (accp, mp, lp))

    def fine_store(r, q0, acc, m, den):
        out = acc / den
        for c in range(p):
            o_ref[pl.ds(q0 + c, sub, stride=p), :] = out[c * sub:(c + 1) * sub]

    run_pass(2, 1,
             load_q=lambda r, q0: jnp.concatenate(
                 [q_ref[pl.ds(q0 + c, sub, stride=p), :] for c in range(p)], axis=0),
             load_kv=lambda grouped, natural, r, k0: natural[pl.ds(pl.multiple_of(k0, 8), win), :],
             load_state=fine_load, store_state=fine_store)


def _dilated_attention(qkv_a, b, t, n_heads):
    n = b * t
    nh = n_heads
    configs = sorted(((dil, window // (2 * dil)) for window, dil in DILATED_CONFIGS), reverse=True)
    (d_wide, _), (p, _), (d_fine, _) = configs
    halves = tuple(half for _, half in configs)
    assert d_fine == 1 and d_wide == p * p and p & (p - 1) == 0 and QBLK % (8 * p) == 0
    assert t % (d_wide * QBLK) == 0 and t // d_wide >= 2 * QBLK and max(halves) <= QBLK // 2
    blk = (None, t, HEAD_DIM)
    rows = pltpu.VMEM((t, LANES), F32)
    return pl.pallas_call(
        functools.partial(_dilated_kernel, t=t, p=p, halves=halves),
        grid=(b, nh),
        in_specs=[pl.BlockSpec(blk, lambda bi, h: (h, bi, 0)),
                  pl.BlockSpec(blk, lambda bi, h: (nh + h, bi, 0)),
                  pl.BlockSpec(blk, lambda bi, h: (2 * nh + h, bi, 0))],
        out_specs=pl.BlockSpec((t, HEAD_DIM), lambda bi, h: (bi, h)),
        out_shape=jax.ShapeDtypeStruct((n, nh * HEAD_DIM), F32),
        scratch_shapes=[rows, rows, rows, rows, rows,
                        pltpu.VMEM((3 * len(halves), QBLK, 2 * QBLK), F32)],
        compiler_params=_params(("parallel", "arbitrary"), 60),
        name="dilated_attn",
    )(qkv_a, qkv_a, qkv_a)


NBR_UNROLL = 32


def _nbr_kernel(q_ref, k_ref, v_ref, bias_ref, o_ref, *, rows):
    scale = HEAD_DIM ** -0.5 * LOG2E
    wk = WIN_ROWS * GRID_W

    def scores(r):
        rs = jnp.clip(r - WIN_ROWS // 2, 0, rows - WIN_ROWS)
        q0 = pl.multiple_of(r * GRID_W, GRID_W)
        k0 = pl.multiple_of(rs * GRID_W, GRID_W)
        q = q_ref[pl.ds(q0, GRID_W), :]
        k = k_ref[pl.ds(k0, wk), :]
        s = lax.dot_general(q, k, (((1,), (1,)), ((), ())), preferred_element_type=F32) * scale
        return q0, k0, s + bias_ref[rs - r + (WIN_ROWS - 1)]

    def softmax(q0, k0, s):
        m = jnp.max(s, axis=-1, keepdims=True)
        p = jnp.exp2(s - m)
        return q0, k0, jnp.sum(p, axis=-1, keepdims=True), p.astype(BF16)

    def values(q0, k0, den, p):
        o = jnp.dot(p, v_ref[pl.ds(k0, wk), :], preferred_element_type=F32)
        o_ref[pl.ds(q0, GRID_W), :] = o / den

    unroll = NBR_UNROLL if rows % NBR_UNROLL == 0 else 1

    def trip(it, carry):
        parts = [scores(it * unroll + u) for u in range(unroll)]
        parts = [softmax(*part) for part in parts]
        for part in parts:
            values(*part)
        return carry

    lax.fori_loop(0, rows // unroll, trip, 0)


def _nbr_bias_tables(rel_bias):
    nh = rel_bias.shape[0]
    c = jnp.arange(GRID_W)
    cs = jnp.clip(c - WIN_COLS // 2, 0, GRID_W - WIN_COLS)
    col_mask = (c[None, :] >= cs[:, None]) & (c[None, :] < cs[:, None] + WIN_COLS)
    dc = jnp.clip(c[None, :] - c[:, None], -(WIN_COLS - 1), WIN_COLS - 1) + (WIN_COLS - 1)
    tab = rel_bias.astype(F32)[:, :, dc]
    tab = jnp.where(col_mask[None, None], tab * LOG2E, NEG_INF)
    idx = jnp.arange(WIN_ROWS)[:, None] + jnp.arange(WIN_ROWS)[None, :]
    tab = tab[:, idx]
    return tab.transpose(0, 1, 3, 2, 4).reshape(nh, WIN_ROWS, GRID_W, WIN_ROWS * GRID_W)


def _nbr_attention(qkvh, bias_tab, b, t, head0, n_heads):
    n = b * t
    rows = t // GRID_W
    assert t % GRID_W == 0 and rows >= WIN_ROWS
    nh = n_heads
    blk = (None, t, HEAD_DIM)
    kern = functools.partial(_nbr_kernel, rows=rows)
    return pl.pallas_call(
        kern,
        grid=(b, nh),
        in_specs=[pl.BlockSpec(blk, lambda bi, h: (head0 + h, bi, 0)),
                  pl.BlockSpec(blk, lambda bi, h: (head0 + nh + h, bi, 0)),
                  pl.BlockSpec(blk, lambda bi, h: (head0 + 2 * nh + h, bi, 0)),
                  pl.BlockSpec((None, WIN_ROWS, GRID_W, WIN_ROWS * GRID_W), lambda bi, h: (h, 0, 0, 0))],
        out_specs=pl.BlockSpec((t, HEAD_DIM), lambda bi, h: (bi, h)),
        out_shape=jax.ShapeDtypeStruct((n, nh * HEAD_DIM), F32),
        compiler_params=_params(("parallel", "arbitrary"), 48),
        name="nbr_attn",
    )(qkvh, qkvh, qkvh, bias_tab)


def _out_norms_kernel(oa_ref, ob_ref, ga_ref, gb_ref, out_ref):
    wa = oa_ref.shape[1]
    out_ref[:, :wa] = _rms(oa_ref[...], ga_ref[...]).astype(out_ref.dtype)
    out_ref[:, wa:] = _rms(ob_ref[...], gb_ref[...]).astype(out_ref.dtype)


def _out_norms(oa, ob, ga, gb):
    n, wa = oa.shape
    wb = ob.shape[1]
    bm = _pick(n, (256, 128, 8))
    return pl.pallas_call(
        _out_norms_kernel,
        grid=(n // bm,),
        in_specs=[pl.BlockSpec((bm, wa), lambda i: (i, 0)),
                  pl.BlockSpec((bm, wb), lambda i: (i, 0)),
                  pl.BlockSpec((1, wa), lambda i: (0, 0)),
                  pl.BlockSpec((1, wb), lambda i: (0, 0))],
        out_specs=pl.BlockSpec((bm, wa + wb), lambda i: (i, 0)),
        out_shape=jax.ShapeDtypeStruct((n, wa + wb), BF16),
        compiler_params=_params(("parallel",), 40),
        name="out_norms",
    )(oa, ob, ga.reshape(1, wa), gb.reshape(1, wb))


def _layer(x, b, t, wts, final_gain):
    (ffn1_norm, ffn1, mix_norm, w_in, bias_tab, out_norm_a, out_norm_b, w_out, ffn2_norm, ffn2) = wts
    wa = out_norm_a.shape[0]
    wb = out_norm_b.shape[0]
    nha, nhb = wa // HEAD_DIM, wb // HEAD_DIM
    rope = _rope_tables(t)

    h1 = _ffn(x, ffn1_norm, *ffn1)
    u = _rmsnorm(h1, mix_norm, BF16)

    qkv_a = _qkv_proj(u, w_in, 0, 3 * wa, rope, t, 2 * wa, F32)
    qkv_b = _qkv_proj(u, w_in, 3 * wa, 3 * wb, rope, t, 0, BF16)
    oa = _dilated_attention(qkv_a, b, t, nha)
    ob = _nbr_attention(qkv_b, bias_tab, b, t, 0, nhb)
    merged = _out_norms(oa, ob, out_norm_a, out_norm_b)
    h2 = _out_proj(merged, w_out, h1)

    return _ffn(h2, ffn2_norm, *ffn2, g_out=final_gain)


def kernel(x_prompt, x_sample, ffn1_norm, ffn1_w_gate, ffn1_w_up, ffn1_w_down, mix_norm, w_in, nbr_rel_bias, out_norm_a, out_norm_b, w_out, ffn2_norm, ffn2_w_gate, ffn2_w_up, ffn2_w_down, final_norm):
    depth = ffn1_norm.shape[0]
    layers = []
    for i in range(depth):
        layers.append((ffn1_norm[i],
                       tuple(_cast_bf16(w, i) for w in (ffn1_w_gate, ffn1_w_up, ffn1_w_down)),
                       mix_norm[i], _cast_bf16(w_in, i), _nbr_bias_tables(nbr_rel_bias[i]),
                       out_norm_a[i], out_norm_b[i], _cast_bf16(w_out, i), ffn2_norm[i],
                       tuple(_cast_bf16(w, i) for w in (ffn2_w_gate, ffn2_w_up, ffn2_w_down))))

    def run(x3):
        b, t, d = x3.shape
        h = x3.reshape(b * t, d)
        for i in range(depth):
            h = _layer(h, b, t, layers[i], final_norm if i == depth - 1 else None)
        return h.reshape(b, t, d)

    return run(x_prompt), run(x_sample)
```

```python
import functools
import math

import jax
import jax.numpy as jnp
from jax import lax
from jax.experimental import pallas as pl
from jax.experimental.pallas import tpu as pltpu

HEAD_DIM = 128
DILATED_CONFIGS = ((128, 1), (512, 4), (2048, 16))
QBLK = 128
ROPE_THETA = 500000.0
ROT_DIM = HEAD_DIM // 4
GRID_W = 64
WIN_ROWS = 8
WIN_COLS = 16
NORM_EPS = 1e-6
NEG_INF = -1e30
LOG2E = math.log2(math.e)
LANES = 128
V7X_VMEM_BYTES = 64 * 1024 * 1024

F32 = jnp.float32
BF16 = jnp.bfloat16


def _params(sem, vmem_mb):
    return pltpu.CompilerParams(dimension_semantics=sem,
                                vmem_limit_bytes=min(vmem_mb * 1024 * 1024, V7X_VMEM_BYTES - (2 << 20)))


def _pick(n, prefs):
    for p in prefs:
        if n % p == 0:
            return p
    return n


def _cast_kernel(w_ref, o_ref):
    o_ref[...] = w_ref[...].astype(o_ref.dtype)


def _cast_bf16(w, layer):
    _, r, c = w.shape
    br = _pick(r, (256, 128, 16))
    return pl.pallas_call(
        _cast_kernel,
        grid=(r // br,),
        in_specs=[pl.BlockSpec((None, br, c), lambda i: (layer, i, 0))],
        out_specs=pl.BlockSpec((br, c), lambda i: (i, 0)),
        out_shape=jax.ShapeDtypeStruct((r, c), BF16),
        compiler_params=_params(("parallel",), 56),
        name="cast_bf16",
    )(w)


def _rms(x, g):
    ms = jnp.mean(x * x, axis=-1, keepdims=True)
    return x * lax.rsqrt(ms + NORM_EPS) * g


def _rmsnorm_kernel(x_ref, g_ref, o_ref):
    o_ref[...] = _rms(x_ref[...], g_ref[...]).astype(o_ref.dtype)


def _rmsnorm(x, g, out_dtype):
    n, d = x.shape
    bm = _pick(n, (256, 128, 8))
    return pl.pallas_call(
        _rmsnorm_kernel,
        grid=(n // bm,),
        in_specs=[pl.BlockSpec((bm, d), lambda i: (i, 0)),
                  pl.BlockSpec((1, d), lambda i: (0, 0))],
        out_specs=pl.BlockSpec((bm, d), lambda i: (i, 0)),
        out_shape=jax.ShapeDtypeStruct((n, d), out_dtype),
        compiler_params=_params(("parallel",), 40),
        name="rmsnorm",
    )(x, g.reshape(1, d))


FFN_ROW_CHUNK = 32


def _ffn_kernel(x_hbm, gin_ref, wg_ref, wu_ref, wd_ref, gout_ref, o_hbm, acc, xn_scr, in_sem, out_sem,
                *, norm_out):
    i, j = pl.program_id(0), pl.program_id(1)
    n_tiles, n_chunks = pl.num_programs(0), pl.num_programs(1)
    _, n_rows, width = acc.shape
    slabs = [slice(c, c + LANES) for c in range(0, width, LANES)]
    slot = lax.rem(i, 2)
    tile = acc.at[slot]

    def x_copy(t, s):
        return pltpu.make_async_copy(x_hbm.at[pl.ds(t * n_rows, n_rows), :], acc.at[s], in_sem.at[s])

    def o_copy(t, s):
        return pltpu.make_async_copy(acc.at[s], o_hbm.at[pl.ds(t * n_rows, n_rows), :], out_sem.at[s])

    def row_loop(fn):
        def trip(c, carry):
            fn(pl.ds(pl.multiple_of(c * FFN_ROW_CHUNK, FFN_ROW_CHUNK), FFN_ROW_CHUNK))
            return carry
        lax.fori_loop(0, n_rows // FFN_ROW_CHUNK, trip, 0, unroll=2)

    def inv_rms(sq_sum):
        return lax.rsqrt(jnp.sum(sq_sum, axis=-1, keepdims=True) / width + NORM_EPS)

    @pl.when(j == 0)
    def _():
        @pl.when(i == 0)
        def _():
            x_copy(0, 0).start()
        x_copy(i, slot).wait()

        def prologue(rows):
            sq = jnp.zeros((FFN_ROW_CHUNK, LANES), F32)
            for sl in slabs:
                v = tile[rows, sl]
                sq = sq + v * v
            inv = inv_rms(sq)
            for sl in slabs:
                xn_scr[rows, sl] = (tile[rows, sl] * inv * gin_ref[:, sl]).astype(BF16)
        row_loop(prologue)

    xn = xn_scr[...]
    g = jnp.dot(xn, wg_ref[...], preferred_element_type=F32)
    u = jnp.dot(xn, wu_ref[...], preferred_element_type=F32)
    a = (0.5 * g * jax.nn.sigmoid(g) * u).astype(BF16)
    tile[...] += jnp.dot(a, wd_ref[...], preferred_element_type=F32)

    @pl.when(j == 1)
    def _():
        @pl.when(i >= 1)
        def _():
            o_copy(i - 1, 1 - slot).wait()

        @pl.when(i + 1 < n_tiles)
        def _():
            x_copy(i + 1, 1 - slot).start()

    @pl.when(j == n_chunks - 1)
    def _():
        if norm_out:
            def epilogue(rows):
                sq = jnp.zeros((FFN_ROW_CHUNK, LANES), F32)
                for sl in slabs:
                    h = tile[rows, sl]
                    sq = sq + h * h
                inv = inv_rms(sq)
                for sl in slabs:
                    tile[rows, sl] = tile[rows, sl] * inv * gout_ref[:, sl]
            row_loop(epilogue)
        o_copy(i, slot).start()

        @pl.when(i == n_tiles - 1)
        def _():
            o_copy(i, slot).wait()


def _ffn(x, g_in, wg, wu, wd, g_out=None):
    n, d = x.shape
    f = wg.shape[1]
    bm = _pick(n, (1024, 512, 256, 128, 32))
    bf = _pick(f, (256, 128))
    assert f % bf == 0 and f // bf >= 2 and bm % (2 * FFN_ROW_CHUNK) == 0
    norm_out = g_out is not None
    gain = pl.BlockSpec((1, d), lambda i, j: (0, 0))
    hbm = pl.BlockSpec(memory_space=pl.ANY)
    dma2 = pltpu.SemaphoreType.DMA((2,))
    return pl.pallas_call(
        functools.partial(_ffn_kernel, norm_out=norm_out),
        grid=(n // bm, f // bf),
        in_specs=[hbm,
                  gain,
                  pl.BlockSpec((d, bf), lambda i, j: (0, j)),
                  pl.BlockSpec((d, bf), lambda i, j: (0, j)),
                  pl.BlockSpec((bf, d), lambda i, j: (j, 0)),
                  gain],
        out_specs=hbm,
        out_shape=jax.ShapeDtypeStruct((n, d), F32),
        scratch_shapes=[pltpu.VMEM((2, bm, d), F32), pltpu.VMEM((bm, d), BF16), dma2, dma2],
        compiler_params=_params(("arbitrary", "arbitrary"), 60),
        name="swiglu_ffn",
    )(x, g_in.reshape(1, d), wg, wu, wd, (g_out if norm_out else g_in).reshape(1, d))


def _qkv_kernel(a_ref, w_ref, *refs, rotary, heads_per_tile):
    o_ref = refs[-1]
    acc = jnp.dot(a_ref[...], w_ref[...], preferred_element_type=F32)
    if rotary:
        c, sa, sb = (r[...] for r in refs[:3])
    for h in range(heads_per_tile):
        x = acc[:, h * HEAD_DIM:(h + 1) * HEAD_DIM]
        if rotary:
            x = (x * c + pltpu.roll(x, HEAD_DIM - ROT_DIM // 2, 1) * sa
                 + pltpu.roll(x, ROT_DIM // 2, 1) * sb)
        o_ref[h] = x.astype(o_ref.dtype)


def _qkv_proj(u, w, col0, m, rope, t, n_rope_cols, out_dtype):
    n, d = u.shape
    bm = _pick(t, (1024, 512, 256, 128))
    bn = next(p for p in (1024, 512, 256, 128)
              if m % p == 0 and n_rope_cols % p == 0 and col0 % p == 0)
    assert n % bm == 0
    tb = t // bm
    j0 = col0 // bn
    rope_tiles = n_rope_cols // bn
    rotary = rope_tiles > 0
    tab = pl.BlockSpec((None, bm, HEAD_DIM), lambda i, j: (jnp.where(j < rope_tiles, 0, 1), i % tb, 0))
    kern = functools.partial(_qkv_kernel, rotary=rotary, heads_per_tile=bn // HEAD_DIM)
    return pl.pallas_call(
        kern,
        grid=(n // bm, m // bn),
        in_specs=[pl.BlockSpec((bm, d), lambda i, j: (i, 0)),
                  pl.BlockSpec((d, bn), lambda i, j: (0, j0 + j))] + ([tab, tab, tab] if rotary else []),
        out_specs=pl.BlockSpec((bn // HEAD_DIM, bm, HEAD_DIM), lambda i, j: (j, i, 0)),
        out_shape=jax.ShapeDtypeStruct((m // HEAD_DIM, n, HEAD_DIM), out_dtype),
        compiler_params=_params(("parallel", "arbitrary"), 56),
        name="qkv_proj",
    )(u, w, *(rope if rotary else ()))


def _out_proj_kernel(a_ref, w_ref, r_ref, o_ref):
    o_ref[...] = r_ref[...] + jnp.dot(a_ref[...], w_ref[...], preferred_element_type=F32)


def _out_proj(a, w, res):
    n, k = a.shape
    m = w.shape[1]
    bm = _pick(n, (1024, 512, 256, 128, 8))
    bn = _pick(m, (1024, 512, 256, 128))
    return pl.pallas_call(
        _out_proj_kernel,
        grid=(n // bm, m // bn),
        in_specs=[pl.BlockSpec((bm, k), lambda i, j: (i, 0)),
                  pl.BlockSpec((k, bn), lambda i, j: (0, j)),
                  pl.BlockSpec((bm, bn), lambda i, j: (i, j))],
        out_specs=pl.BlockSpec((bm, bn), lambda i, j: (i, j)),
        out_shape=jax.ShapeDtypeStruct((n, m), F32),
        compiler_params=_params(("parallel", "arbitrary"), 56),
        name="out_proj",
    )(a, w, res)


def _rope_tables(t):
    pos = jnp.arange(t, dtype=F32)
    inv = ROPE_THETA ** (-jnp.arange(0, ROT_DIM, 2, dtype=F32) / ROT_DIM)
    ang = pos[:, None] * inv[None, :]
    cos, sin = jnp.cos(ang), jnp.sin(ang)
    half = ROT_DIM // 2
    z_half = jnp.zeros((t, half), F32)
    z_rest = jnp.zeros((t, HEAD_DIM - ROT_DIM), F32)
    c = jnp.concatenate([cos, cos, jnp.ones((t, HEAD_DIM - ROT_DIM), F32)], axis=1)
    sa = jnp.concatenate([-sin, z_half, z_rest], axis=1)
    sb = jnp.concatenate([z_half, sin, z_rest], axis=1)
    return (jnp.stack([c, jnp.ones_like(c)]), jnp.stack([sa, jnp.zeros_like(sa)]),
            jnp.stack([sb, jnp.zeros_like(sb)]))


DIL_UNROLL = 8


def _dilated_kernel(q_ref, k_ref, v_ref, o_ref, kp, vp, accp, mp, lp, mask_scr, *, t, p, halves):
    scale = HEAD_DIM ** -0.5 * LOG2E
    win = 2 * QBLK
    lgrp = t // p
    sub = QBLK // p
    log_p, log_sub = p.bit_length() - 1, sub.bit_length() - 1
    qi = lax.broadcasted_iota(jnp.int32, (QBLK, win), 0)
    ki = lax.broadcasted_iota(jnp.int32, (QBLK, win), 1)
    qi_regrouped = ((qi & (sub - 1)) << log_p) + (qi >> log_sub)

    for ci, half in enumerate(halves):
        rows_pos = qi_regrouped if ci == 2 else qi
        for vi, off in enumerate((0, half, QBLK)):
            mask_scr[3 * ci + vi] = jnp.where(jnp.abs(rows_pos - ki + off) <= half, 0.0, NEG_INF)

    def regroup(c, carry):
        r = lax.div(c, lgrp // QBLK)
        src = pl.ds(r + p * (c * QBLK - r * lgrp), QBLK, stride=p)
        dst = pl.ds(pl.multiple_of(c * QBLK, QBLK), QBLK)
        kp[dst, :] = k_ref[src, :]
        vp[dst, :] = v_ref[src, :]
        return carry

    lax.fori_loop(0, t // QBLK, regroup, 0, unroll=2)

    def softmax(blk, s):
        m = jnp.max(s, axis=-1, keepdims=True)
        e = jnp.exp2(s - m)
        return blk, m, jnp.sum(e, axis=-1, keepdims=True), e.astype(BF16)

    def run_pass(ci, d, load_q, load_kv, load_state, store_state):
        half = halves[ci]
        l = t // d
        nblk = l // QBLK
        shift = d.bit_length() - 1

        def scores(idx):
            i = lax.shift_right_logical(idx, shift)
            r = idx - (i << shift)
            q0 = i * QBLK
            k0 = jnp.clip(q0 - half, 0, l - win)
            placement = jnp.where(i == 0, 0, jnp.where(i == nblk - 1, 2, 1))
            q = load_q(r, q0).astype(BF16)
            k = load_kv(kp, k_ref, r, k0).astype(BF16)
            s = lax.dot_general(q, k, (((1,), (1,)), ((), ())), preferred_element_type=F32) * scale
            return (r, q0, k0), s + mask_scr[3 * ci + placement]

        def values(blk, m, den, e):
            r, _, k0 = blk
            v = load_kv(vp, v_ref, r, k0).astype(BF16)
            return blk, m, den, jnp.dot(e, v, preferred_element_type=F32)

        def merge(blk, m, den, acc):
            r, q0, _ = blk
            if ci == 0:
                store_state(r, q0, acc, jnp.broadcast_to(m, (QBLK, LANES)),
                            jnp.broadcast_to(den, (QBLK, LANES)))
                return
            acc_run, m_run, l_run = load_state(r, q0)
            m_new = jnp.maximum(m_run, m)
            a_run = jnp.exp2(m_run - m_new)
            a_blk = jnp.exp2(m - m_new)
            store_state(r, q0, acc_run * a_run + acc * a_blk, m_new, l_run * a_run + den * a_blk)

        total = d * nblk
        unroll = DIL_UNROLL if total % DIL_UNROLL == 0 else 1

        def trip(it, carry):
            parts = [scores(it * unroll + u) for u in range(unroll)]
            parts = [softmax(*part) for part in parts]
            parts = [values(*part) for part in parts]
            for part in parts:
                merge(*part)
            return carry

        lax.fori_loop(0, total // unroll, trip, 0)

    def store_regrouped(rows, acc, m, den):
        accp[rows, :] = acc
        mp[rows, :] = m
        lp[rows, :] = den

    def wide_rows(r, start, size):
        rr = lax.shift_right_logical(r, log_p)
        return pl.ds((r - (rr << log_p)) * lgrp + rr + p * start, size, stride=p)

    run_pass(0, p * p,
             load_q=lambda r, q0: q_ref[pl.ds(r + p * p * q0, QBLK, stride=p * p), :],
             load_kv=lambda grouped, natural, r, k0: grouped[wide_rows(r, k0, win), :],
             load_state=None,
             store_state=lambda r, q0, *state: store_regrouped(wide_rows(r, q0, QBLK), *state))

    def mid_rows(r, start, size):
        return pl.ds(pl.multiple_of(r * lgrp + start, 8), size)

    def mid_load(r, q0):
        rows = mid_rows(r, q0, QBLK)
        return accp[rows, :], mp[rows, :], lp[rows, :]

    run_pass(1, p,
             load_q=lambda r, q0: q_ref[pl.ds(r + p * q0, QBLK, stride=p), :],
             load_kv=lambda grouped, natural, r, k0: grouped[mid_rows(r, k0, win), :],
             load_state=mid_load,
             store_state=lambda r, q0, *state: store_regrouped(mid_rows(r, q0, QBLK), *state))

    def fine_load(r, q0):
        rows = [pl.ds(pl.multiple_of(c * lgrp + (q0 >> log_p), 8), sub) for c in range(p)]
        return tuple(jnp.concatenate([ref[rc, :] for rc in rows], axis=0) for ref in (accp, mp, lp))

    def fine_store(r, q0, acc, m, den):
        out = acc / den
        for c in range(p):
            o_ref[pl.ds(q0 + c, sub, stride=p), :] = out[c * sub:(c + 1) * sub]

    run_pass(2, 1,
             load_q=lambda r, q0: jnp.concatenate(
                 [q_ref[pl.ds(q0 + c, sub, stride=p), :] for c in range(p)], axis=0),
             load_kv=lambda grouped, natural, r, k0: natural[pl.ds(pl.multiple_of(k0, 8), win), :],
             load_state=fine_load, store_state=fine_store)


def _dilated_attention(qkv_a, b, t, n_heads):
    n = b * t
    nh = n_heads
    configs = sorted(((dil, window // (2 * dil)) for window, dil in DILATED_CONFIGS), reverse=True)
    (d_wide, _), (p, _), (d_fine, _) = configs
    halves = tuple(half for _, half in configs)
    assert d_fine == 1 and d_wide == p * p and p & (p - 1) == 0 and QBLK % (8 * p) == 0
    assert t % (d_wide * QBLK) == 0 and t // d_wide >= 2 * QBLK and max(halves) <= QBLK // 2
    blk = (None, t, HEAD_DIM)
    rows = pltpu.VMEM((t, LANES), F32)
    return pl.pallas_call(
        functools.partial(_dilated_kernel, t=t, p=p, halves=halves),
        grid=(b, nh),
        in_specs=[pl.BlockSpec(blk, lambda bi, h: (h, bi, 0)),
                  pl.BlockSpec(blk, lambda bi, h: (nh + h, bi, 0)),
                  pl.BlockSpec(blk, lambda bi, h: (2 * nh + h, bi, 0))],
        out_specs=pl.BlockSpec((t, HEAD_DIM), lambda bi, h: (bi, h)),
        out_shape=jax.ShapeDtypeStruct((n, nh * HEAD_DIM), F32),
        scratch_shapes=[rows, rows, rows, rows, rows,
                        pltpu.VMEM((3 * len(halves), QBLK, 2 * QBLK), F32)],
        compiler_params=_params(("parallel", "arbitrary"), 60),
        name="dilated_attn",
    )(qkv_a, qkv_a, qkv_a)


NBR_UNROLL = 32


def _nbr_kernel(q_ref, k_ref, v_ref, bias_ref, o_ref, *, rows):
    scale = HEAD_DIM ** -0.5 * LOG2E
    wk = WIN_ROWS * GRID_W

    def scores(r):
        rs = jnp.clip(r - WIN_ROWS // 2, 0, rows - WIN_ROWS)
        q0 = pl.multiple_of(r * GRID_W, GRID_W)
        k0 = pl.multiple_of(rs * GRID_W, GRID_W)
        q = q_ref[pl.ds(q0, GRID_W), :]
        k = k_ref[pl.ds(k0, wk), :]
        s = lax.dot_general(q, k, (((1,), (1,)), ((), ())), preferred_element_type=F32) * scale
        return q0, k0, s + bias_ref[rs - r + (WIN_ROWS - 1)]

    def softmax(q0, k0, s):
        m = jnp.max(s, axis=-1, keepdims=True)
        p = jnp.exp2(s - m)
        return q0, k0, jnp.sum(p, axis=-1, keepdims=True), p.astype(BF16)

    def values(q0, k0, den, p):
        o = jnp.dot(p, v_ref[pl.ds(k0, wk), :], preferred_element_type=F32)
        o_ref[pl.ds(q0, GRID_W), :] = o / den

    unroll = NBR_UNROLL if rows % NBR_UNROLL == 0 else 1

    def trip(it, carry):
        parts = [scores(it * unroll + u) for u in range(unroll)]
        parts = [softmax(*part) for part in parts]
        for part in parts:
            values(*part)
        return carry

    lax.fori_loop(0, rows // unroll, trip, 0)


def _nbr_bias_tables(rel_bias):
    nh = rel_bias.shape[0]
    c = jnp.arange(GRID_W)
    cs = jnp.clip(c - WIN_COLS // 2, 0, GRID_W - WIN_COLS)
    col_mask = (c[None, :] >= cs[:, None]) & (c[None, :] < cs[:, None] + WIN_COLS)
    dc = jnp.clip(c[None, :] - c[:, None], -(WIN_COLS - 1), WIN_COLS - 1) + (WIN_COLS - 1)
    tab = rel_bias.astype(F32)[:, :, dc]
    tab = jnp.where(col_mask[None, None], tab * LOG2E, NEG_INF)
    idx = jnp.arange(WIN_ROWS)[:, None] + jnp.arange(WIN_ROWS)[None, :]
    tab = tab[:, idx]
    return tab.transpose(0, 1, 3, 2, 4).reshape(nh, WIN_ROWS, GRID_W, WIN_ROWS * GRID_W)


def _nbr_attention(qkvh, bias_tab, b, t, head0, n_heads):
    n = b * t
    rows = t // GRID_W
    assert t % GRID_W == 0 and rows >= WIN_ROWS
    nh = n_heads
    blk = (None, t, HEAD_DIM)
    kern = functools.partial(_nbr_kernel, rows=rows)
    return pl.pallas_call(
        kern,
        grid=(b, nh),
        in_specs=[pl.BlockSpec(blk, lambda bi, h: (head0 + h, bi, 0)),
                  pl.BlockSpec(blk, lambda bi, h: (head0 + nh + h, bi, 0)),
                  pl.BlockSpec(blk, lambda bi, h: (head0 + 2 * nh + h, bi, 0)),
                  pl.BlockSpec((None, WIN_ROWS, GRID_W, WIN_ROWS * GRID_W), lambda bi, h: (h, 0, 0, 0))],
        out_specs=pl.BlockSpec((t, HEAD_DIM), lambda bi, h: (bi, h)),
        out_shape=jax.ShapeDtypeStruct((n, nh * HEAD_DIM), F32),
        compiler_params=_params(("parallel", "arbitrary"), 48),
        name="nbr_attn",
    )(qkvh, qkvh, qkvh, bias_tab)


def _out_norms_kernel(oa_ref, ob_ref, ga_ref, gb_ref, out_ref):
    wa = oa_ref.shape[1]
    out_ref[:, :wa] = _rms(oa_ref[...], ga_ref[...]).astype(out_ref.dtype)
    out_ref[:, wa:] = _rms(ob_ref[...], gb_ref[...]).astype(out_ref.dtype)


def _out_norms(oa, ob, ga, gb):
    n, wa = oa.shape
    wb = ob.shape[1]
    bm = _pick(n, (256, 128, 8))
    return pl.pallas_call(
        _out_norms_kernel,
        grid=(n // bm,),
        in_specs=[pl.BlockSpec((bm, wa), lambda i: (i, 0)),
                  pl.BlockSpec((bm, wb), lambda i: (i, 0)),
                  pl.BlockSpec((1, wa), lambda i: (0, 0)),
                  pl.BlockSpec((1, wb), lambda i: (0, 0))],
        out_specs=pl.BlockSpec((bm, wa + wb), lambda i: (i, 0)),
        out_shape=jax.ShapeDtypeStruct((n, wa + wb), BF16),
        compiler_params=_params(("parallel",), 40),
        name="out_norms",
    )(oa, ob, ga.reshape(1, wa), gb.reshape(1, wb))


def _layer(x, b, t, wts, final_gain):
    (ffn1_norm, ffn1, mix_norm, w_in, bias_tab, out_norm_a, out_norm_b, w_out, ffn2_norm, ffn2) = wts
    wa = out_norm_a.shape[0]
    wb = out_norm_b.shape[0]
    nha, nhb = wa // HEAD_DIM, wb // HEAD_DIM
    rope = _rope_tables(t)

    h1 = _ffn(x, ffn1_norm, *ffn1)
    u = _rmsnorm(h1, mix_norm, BF16)

    qkv_a = _qkv_proj(u, w_in, 0, 3 * wa, rope, t, 2 * wa, F32)
    qkv_b = _qkv_proj(u, w_in, 3 * wa, 3 * wb, rope, t, 0, BF16)
    oa = _dilated_attention(qkv_a, b, t, nha)
    ob = _nbr_attention(qkv_b, bias_tab, b, t, 0, nhb)
    merged = _out_norms(oa, ob, out_norm_a, out_norm_b)
    h2 = _out_proj(merged, w_out, h1)

    return _ffn(h2, ffn2_norm, *ffn2, g_out=final_gain)


def kernel(x_prompt, x_sample, ffn1_norm, ffn1_w_gate, ffn1_w_up, ffn1_w_down, mix_norm, w_in, nbr_rel_bias, out_norm_a, out_norm_b, w_out, ffn2_norm, ffn2_w_gate, ffn2_w_up, ffn2_w_down, final_norm):
    depth = ffn1_norm.shape[0]
    layers = []
    for i in range(depth):
        layers.append((ffn1_norm[i],
                       tuple(_cast_bf16(w, i) for w in (ffn1_w_gate, ffn1_w_up, ffn1_w_down)),
                       mix_norm[i], _cast_bf16(w_in, i), _nbr_bias_tables(nbr_rel_bias[i]),
                       out_norm_a[i], out_norm_b[i], _cast_bf16(w_out, i), ffn2_norm[i],
                       tuple(_cast_bf16(w, i) for w in (ffn2_w_gate, ffn2_w_up, ffn2_w_down))))

    def run(x3):
        b, t, d = x3.shape
        h = x3.reshape(b * t, d)
        for i in range(depth):
            h = _layer(h, b, t, layers[i], final_norm if i == depth - 1 else None)
        return h.reshape(b, t, d)

    return run(x_prompt), run(x_sample)
```

```python
import functools
import math

import jax
import jax.numpy as jnp
from jax import lax
from jax.experimental import pallas as pl
from jax.experimental.pallas import tpu as pltpu

HEAD_DIM = 128
DILATED_CONFIGS = ((128, 1), (512, 4), (2048, 16))
QBLK = 128
ROPE_THETA = 500000.0
ROT_DIM = HEAD_DIM // 4
GRID_W = 64
WIN_ROWS = 8
WIN_COLS = 16
NORM_EPS = 1e-6
NEG_INF = -1e30
LOG2E = math.log2(math.e)
LANES = 128
V7X_VMEM_BYTES = 64 * 1024 * 1024

F32 = jnp.float32
BF16 = jnp.bfloat16


def _params(sem, vmem_mb):
    return pltpu.CompilerParams(dimension_semantics=sem,
                                vmem_limit_bytes=min(vmem_mb * 1024 * 1024, V7X_VMEM_BYTES - (2 << 20)))


def _pick(n, prefs):
    for p in prefs:
        if n % p == 0:
            return p
    return n


def _cast_kernel(w_ref, o_ref):
    o_ref[...] = w_ref[...].astype(o_ref.dtype)


def _cast_bf16(w, layer):
    _, r, c = w.shape
    br = _pick(r, (256, 128, 16))
    return pl.pallas_call(
        _cast_kernel,
        grid=(r // br,),
        in_specs=[pl.BlockSpec((None, br, c), lambda i: (layer, i, 0))],
        out_specs=pl.BlockSpec((br, c), lambda i: (i, 0)),
        out_shape=jax.ShapeDtypeStruct((r, c), BF16),
        compiler_params=_params(("parallel",), 56),
        name="cast_bf16",
    )(w)


def _rms(x, g):
    ms = jnp.mean(x * x, axis=-1, keepdims=True)
    return x * lax.rsqrt(ms + NORM_EPS) * g


def _rmsnorm_kernel(x_ref, g_ref, o_ref):
    o_ref[...] = _rms(x_ref[...], g_ref[...]).astype(o_ref.dtype)


def _rmsnorm(x, g, out_dtype):
    n, d = x.shape
    bm = _pick(n, (256, 128, 8))
    return pl.pallas_call(
        _rmsnorm_kernel,
        grid=(n // bm,),
        in_specs=[pl.BlockSpec((bm, d), lambda i: (i, 0)),
                  pl.BlockSpec((1, d), lambda i: (0, 0))],
        out_specs=pl.BlockSpec((bm, d), lambda i: (i, 0)),
        out_shape=jax.ShapeDtypeStruct((n, d), out_dtype),
        compiler_params=_params(("parallel",), 40),
        name="rmsnorm",
    )(x, g.reshape(1, d))


FFN_ROW_CHUNK = 32


def _ffn_kernel(x_hbm, gin_ref, wg_ref, wu_ref, wd_ref, gout_ref, *rest, norm_out, n_cast):
    cast_in, (o_hbm, *cast_out) = rest[:n_cast], rest[n_cast:2 * n_cast + 1]
    acc, xn_scr, in_sem, out_sem = rest[2 * n_cast + 1:]
    i, j = pl.program_id(0), pl.program_id(1)
    n_tiles, n_chunks = pl.num_programs(0), pl.num_programs(1)
    _, n_rows, width = acc.shape
    slabs = [slice(c, c + LANES) for c in range(0, width, LANES)]
    slot = lax.rem(i, 2)
    tile = acc.at[slot]

    def x_copy(t, s):
        return pltpu.make_async_copy(x_hbm.at[pl.ds(t * n_rows, n_rows), :], acc.at[s], in_sem.at[s])

    def o_copy(t, s):
        return pltpu.make_async_copy(acc.at[s], o_hbm.at[pl.ds(t * n_rows, n_rows), :], out_sem.at[s])

    def row_loop(fn):
        def trip(c, carry):
            fn(pl.ds(pl.multiple_of(c * FFN_ROW_CHUNK, FFN_ROW_CHUNK), FFN_ROW_CHUNK))
            return carry
        lax.fori_loop(0, n_rows // FFN_ROW_CHUNK, trip, 0, unroll=2)

    def inv_rms(sq_sum):
        return lax.rsqrt(jnp.sum(sq_sum, axis=-1, keepdims=True) / width + NORM_EPS)

    @pl.when(j == 0)
    def _():
        @pl.when(i == 0)
        def _():
            x_copy(0, 0).start()
        x_copy(i, slot).wait()

        def prologue(rows):
            sq = jnp.zeros((FFN_ROW_CHUNK, LANES), F32)
            for sl in slabs:
                v = tile[rows, sl]
                sq = sq + v * v
            inv = inv_rms(sq)
            for sl in slabs:
                xn_scr[rows, sl] = (tile[rows, sl] * inv * gin_ref[:, sl]).astype(BF16)
        row_loop(prologue)

    xn = xn_scr[...]
    g = jnp.dot(xn, wg_ref[...], preferred_element_type=F32)
    u = jnp.dot(xn, wu_ref[...], preferred_element_type=F32)
    a = (0.5 * g * jax.nn.sigmoid(g) * u).astype(BF16)
    tile[...] += jnp.dot(a, wd_ref[...], preferred_element_type=F32)
    for src, dst in zip(cast_in, cast_out):
        dst[...] = src[...].astype(BF16)

    @pl.when(j == 1)
    def _():
        @pl.when(i >= 1)
        def _():
            o_copy(i - 1, 1 - slot).wait()

        @pl.when(i + 1 < n_tiles)
        def _():
            x_copy(i + 1, 1 - slot).start()

    @pl.when(j == n_chunks - 1)
    def _():
        if norm_out:
            def epilogue(rows):
                sq = jnp.zeros((FFN_ROW_CHUNK, LANES), F32)
                for sl in slabs:
                    h = tile[rows, sl]
                    sq = sq + h * h
                inv = inv_rms(sq)
                for sl in slabs:
                    tile[rows, sl] = tile[rows, sl] * inv * gout_ref[:, sl]
            row_loop(epilogue)
        o_copy(i, slot).start()

        @pl.when(i == n_tiles - 1)
        def _():
            o_copy(i, slot).wait()


CAST_BLOCKS = ((512, 256), (256, 512), (512, 512), (512, 1024))


def _cast_block(rows, cols, n_steps):
    for br, bc in CAST_BLOCKS:
        if rows % br == 0 and cols % bc == 0 and (rows // br) * (cols // bc) <= n_steps:
            return br, bc
    return None


def _ffn_steps(n, f):
    bm = _pick(n, (1024, 512, 256, 128, 32))
    bf = _pick(f, (256, 128))
    return bm, bf, (n // bm) * (f // bf)


def _ffn(x, g_in, wg, wu, wd, g_out=None, cast_next=()):
    n, d = x.shape
    f = wg.shape[1]
    bm, bf, n_steps = _ffn_steps(n, f)
    n_tiles, n_chunks = n // bm, f // bf
    assert f % bf == 0 and n_chunks >= 2 and bm % (2 * FFN_ROW_CHUNK) == 0
    norm_out = g_out is not None
    gain = pl.BlockSpec((1, d), lambda i, j: (0, 0))
    hbm = pl.BlockSpec(memory_space=pl.ANY)
    dma2 = pltpu.SemaphoreType.DMA((2,))
    cast_in_specs, cast_out_specs, cast_shapes, cast_args = [], [], [], []
    for w, layer in cast_next:
        _, rows, cols = w.shape
        blk = _cast_block(rows, cols, n_steps)
        assert blk is not None
        n_row, n_col = rows // blk[0], cols // blk[1]

        if n_col == n_chunks:
            def pos(i, j, n_row=n_row, n_col=n_col):
                return jnp.minimum(i, n_row - 1), jnp.where(i < n_row, j, n_col - 1)
        elif n_row == n_chunks:
            def pos(i, j, n_row=n_row, n_col=n_col):
                return jnp.where(i < n_col, j, n_row - 1), jnp.minimum(i, n_col - 1)
        else:
            def pos(i, j, n_row=n_row, n_col=n_col):
                unit = jnp.minimum(i * n_chunks + j, n_row * n_col - 1)
                return unit // n_col, unit % n_col

        cast_in_specs.append(pl.BlockSpec((None,) + blk, lambda i, j, pos=pos, layer=layer: (layer,) + pos(i, j)))
        cast_out_specs.append(pl.BlockSpec(blk, pos))
        cast_shapes.append(jax.ShapeDtypeStruct((rows, cols), BF16))
        cast_args.append(w)
    main = jax.ShapeDtypeStruct((n, d), F32)
    out = pl.pallas_call(
        functools.partial(_ffn_kernel, norm_out=norm_out, n_cast=len(cast_args)),
        grid=(n_tiles, n_chunks),
        in_specs=[hbm,
                  gain,
                  pl.BlockSpec((d, bf), lambda i, j: (0, j)),
                  pl.BlockSpec((d, bf), lambda i, j: (0, j)),
                  pl.BlockSpec((bf, d), lambda i, j: (j, 0)),
                  gain] + cast_in_specs,
        out_specs=[hbm] + cast_out_specs if cast_args else hbm,
        out_shape=[main] + cast_shapes if cast_args else main,
        scratch_shapes=[pltpu.VMEM((2, bm, d), F32), pltpu.VMEM((bm, d), BF16), dma2, dma2],
        compiler_params=_params(("arbitrary", "arbitrary"), 60),
        name="swiglu_ffn",
    )(x, g_in.reshape(1, d), wg, wu, wd, (g_out if norm_out else g_in).reshape(1, d), *cast_args)
    return (out[0], tuple(out[1:])) if cast_args else out


def _qkv_kernel(a_ref, w_ref, *refs, rotary, heads_per_tile):
    o_ref = refs[-1]
    acc = jnp.dot(a_ref[...], w_ref[...], preferred_element_type=F32)
    if rotary:
        c, sa, sb = (r[...] for r in refs[:3])
    for h in range(heads_per_tile):
        x = acc[:, h * HEAD_DIM:(h + 1) * HEAD_DIM]
        if rotary:
            x = (x * c + pltpu.roll(x, HEAD_DIM - ROT_DIM // 2, 1) * sa
                 + pltpu.roll(x, ROT_DIM // 2, 1) * sb)
        o_ref[h] = x.astype(o_ref.dtype)


def _qkv_proj(u, w, col0, m, rope, t, n_rope_cols, out_dtype):
    n, d = u.shape
    bm = _pick(t, (1024, 512, 256, 128))
    bn = next(p for p in (1024, 512, 256, 128)
              if m % p == 0 and n_rope_cols % p == 0 and col0 % p == 0)
    assert n % bm == 0
    tb = t // bm
    j0 = col0 // bn
    rope_tiles = n_rope_cols // bn
    rotary = rope_tiles > 0
    tab = pl.BlockSpec((None, bm, HEAD_DIM), lambda i, j: (jnp.where(j < rope_tiles, 0, 1), i % tb, 0))
    kern = functools.partial(_qkv_kernel, rotary=rotary, heads_per_tile=bn // HEAD_DIM)
    return pl.pallas_call(
        kern,
        grid=(n // bm, m // bn),
        in_specs=[pl.BlockSpec((bm, d), lambda i, j: (i, 0)),
                  pl.BlockSpec((d, bn), lambda i, j: (0, j0 + j))] + ([tab, tab, tab] if rotary else []),
        out_specs=pl.BlockSpec((bn // HEAD_DIM, bm, HEAD_DIM), lambda i, j: (j, i, 0)),
        out_shape=jax.ShapeDtypeStruct((m // HEAD_DIM, n, HEAD_DIM), out_dtype),
        compiler_params=_params(("parallel", "arbitrary"), 56),
        name="qkv_proj",
    )(u, w, *(rope if rotary else ()))


def _out_proj_kernel(a_ref, w_ref, r_ref, o_ref):
    o_ref[...] = r_ref[...] + jnp.dot(a_ref[...], w_ref[...], preferred_element_type=F32)


def _out_proj(a, w, res):
    n, k = a.shape
    m = w.shape[1]
    bm = _pick(n, (1024, 512, 256, 128, 8))
    bn = _pick(m, (1024, 512, 256, 128))
    return pl.pallas_call(
        _out_proj_kernel,
        grid=(n // bm, m // bn),
        in_specs=[pl.BlockSpec((bm, k), lambda i, j: (i, 0)),
                  pl.BlockSpec((k, bn), lambda i, j: (0, j)),
                  pl.BlockSpec((bm, bn), lambda i, j: (i, j))],
        out_specs=pl.BlockSpec((bm, bn), lambda i, j: (i, j)),
        out_shape=jax.ShapeDtypeStruct((n, m), F32),
        compiler_params=_params(("parallel", "arbitrary"), 56),
        name="out_proj",
    )(a, w, res)


def _rope_tables(t):
    pos = jnp.arange(t, dtype=F32)
    inv = ROPE_THETA ** (-jnp.arange(0, ROT_DIM, 2, dtype=F32) / ROT_DIM)
    ang = pos[:, None] * inv[None, :]
    cos, sin = jnp.cos(ang), jnp.sin(ang)
    half = ROT_DIM // 2
    z_half = jnp.zeros((t, half), F32)
    z_rest = jnp.zeros((t, HEAD_DIM - ROT_DIM), F32)
    c = jnp.concatenate([cos, cos, jnp.ones((t, HEAD_DIM - ROT_DIM), F32)], axis=1)
    sa = jnp.concatenate([-sin, z_half, z_rest], axis=1)
    sb = jnp.concatenate([z_half, sin, z_rest], axis=1)
    return (jnp.stack([c, jnp.ones_like(c)]), jnp.stack([sa, jnp.zeros_like(sa)]),
            jnp.stack([sb, jnp.zeros_like(sb)]))


DIL_UNROLL = 8


def _dilated_kernel(q_ref, k_ref, v_ref, o_ref, kp, vp, accp, mp, lp, mask_scr, *, t, p, halves):
    scale = HEAD_DIM ** -0.5 * LOG2E
    win = 2 * QBLK
    lgrp = t // p
    sub = QBLK // p
    log_p, log_sub = p.bit_length() - 1, sub.bit_length() - 1
    qi = lax.broadcasted_iota(jnp.int32, (QBLK, win), 0)
    ki = lax.broadcasted_iota(jnp.int32, (QBLK, win), 1)
    qi_regrouped = ((qi & (sub - 1)) << log_p) + (qi >> log_sub)

    for ci, half in enumerate(halves):
        rows_pos = qi_regrouped if ci == 2 else qi
        for vi, off in enumerate((0, half, QBLK)):
            mask_scr[3 * ci + vi] = jnp.where(jnp.abs(rows_pos - ki + off) <= half, 0.0, NEG_INF)

    def regroup(c, carry):
        r = lax.div(c, lgrp // QBLK)
        src = pl.ds(r + p * (c * QBLK - r * lgrp), QBLK, stride=p)
        dst = pl.ds(pl.multiple_of(c * QBLK, QBLK), QBLK)
        kp[dst, :] = k_ref[src, :]
        vp[dst, :] = v_ref[src, :]
        return carry

    lax.fori_loop(0, t // QBLK, regroup, 0, unroll=2)

    def softmax(blk, s):
        m = jnp.max(s, axis=-1, keepdims=True)
        e = jnp.exp2(s - m)
        return blk, m, jnp.sum(e, axis=-1, keepdims=True), e.astype(BF16)

    def run_pass(ci, d, load_q, load_kv, load_state, store_state):
        half = halves[ci]
        l = t // d
        nblk = l // QBLK
        shift = d.bit_length() - 1

        def scores(idx):
            i = lax.shift_right_logical(idx, shift)
            r = idx - (i << shift)
            q0 = i * QBLK
            k0 = jnp.clip(q0 - half, 0, l - win)
            placement = jnp.where(i == 0, 0, jnp.where(i == nblk - 1, 2, 1))
            q = load_q(r, q0).astype(BF16)
            k = load_kv(kp, k_ref, r, k0).astype(BF16)
            s = lax.dot_general(q, k, (((1,), (1,)), ((), ())), preferred_element_type=F32) * scale
            return (r, q0, k0), s + mask_scr[3 * ci + placement]

        def values(blk, m, den, e):
            r, _, k0 = blk
            v = load_kv(vp, v_ref, r, k0).astype(BF16)
            return blk, m, den, jnp.dot(e, v, preferred_element_type=F32)

        def merge(blk, m, den, acc):
            r, q0, _ = blk
            if ci == 0:
                store_state(r, q0, acc, jnp.broadcast_to(m, (QBLK, LANES)),
                            jnp.broadcast_to(den, (QBLK, LANES)))
                return
            acc_run, m_run, l_run = load_state(r, q0)
            m_new = jnp.maximum(m_run, m)
            a_run = jnp.exp2(m_run - m_new)
            a_blk = jnp.exp2(m - m_new)
            store_state(r, q0, acc_run * a_run + acc * a_blk, m_new, l_run * a_run + den * a_blk)

        total = d * nblk
        unroll = DIL_UNROLL if total % DIL_UNROLL == 0 else 1

        def trip(it, carry):
            parts = [scores(it * unroll + u) for u in range(unroll)]
            parts = [softmax(*part) for part in parts]
            parts = [values(*part) for part in parts]
            for part in parts:
                merge(*part)
            return carry

        lax.fori_loop(0, total // unroll, trip, 0)

    def store_regrouped(rows, acc, m, den):
        accp[rows, :] = acc
        mp[rows, :] = m
        lp[rows, :] = den

    def wide_rows(r, start, size):
        rr = lax.shift_right_logical(r, log_p)
        return pl.ds((r - (rr << log_p)) * lgrp + rr + p * start, size, stride=p)

    run_pass(0, p * p,
             load_q=lambda r, q0: q_ref[pl.ds(r + p * p * q0, QBLK, stride=p * p), :],
             load_kv=lambda grouped, natural, r, k0: grouped[wide_rows(r, k0, win), :],
             load_state=None,
             store_state=lambda r, q0, *state: store_regrouped(wide_rows(r, q0, QBLK), *state))

    def mid_rows(r, start, size):
        return pl.ds(pl.multiple_of(r * lgrp + start, 8), size)

    def mid_load(r, q0):
        rows = mid_rows(r, q0, QBLK)
        return accp[rows, :], mp[rows, :], lp[rows, :]

    run_pass(1, p,
             load_q=lambda r, q0: q_ref[pl.ds(r + p * q0, QBLK, stride=p), :],
             load_kv=lambda grouped, natural, r, k0: grouped[mid_rows(r, k0, win), :],
             load_state=mid_load,
             store_state=lambda r, q0, *state: store_regrouped(mid_rows(r, q0, QBLK), *state))

    def fine_load(r, q0):
        rows = [pl.ds(pl.multiple_of(c * lgrp + (q0 >> log_p), 8), sub) for c in range(p)]
        return tuple(jnp.concatenate([ref[rc, :] for rc in rows], axis=0) for ref in (accp, mp, lp))

    def fine_store(r, q0, acc, m, den):
        out = acc / den
        for c in range(p):
            o_ref[pl.ds(q0 + c, sub, stride=p), :] = out[c * sub:(c + 1) * sub]

    run_pass(2, 1,
             load_q=lambda r, q0: jnp.concatenate(
                 [q_ref[pl.ds(q0 + c, sub, stride=p), :] for c in range(p)], axis=0),
             load_kv=lambda grouped, natural, r, k0: natural[pl.ds(pl.multiple_of(k0, 8), win), :],
             load_state=fine_load, store_state=fine_store)


def _dilated_attention(qkv_a, b, t, n_heads):
    n = b * t
    nh = n_heads
    configs = sorted(((dil, window // (2 * dil)) for window, dil in DILATED_CONFIGS), reverse=True)
    (d_wide, _), (p, _), (d_fine, _) = configs
    halves = tuple(half for _, half in configs)
    assert d_fine == 1 and d_wide == p * p and p & (p - 1) == 0 and QBLK % (8 * p) == 0
    assert t % (d_wide * QBLK) == 0 and t // d_wide >= 2 * QBLK and max(halves) <= QBLK // 2
    blk = (None, t, HEAD_DIM)
    rows = pltpu.VMEM((t, LANES), F32)
    return pl.pallas_call(
        functools.partial(_dilated_kernel, t=t, p=p, halves=halves),
        grid=(b, nh),
        in_specs=[pl.BlockSpec(blk, lambda bi, h: (h, bi, 0)),
                  pl.BlockSpec(blk, lambda bi, h: (nh + h, bi, 0)),
                  pl.BlockSpec(blk, lambda bi, h: (2 * nh + h, bi, 0))],
        out_specs=pl.BlockSpec((t, HEAD_DIM), lambda bi, h: (bi, h)),
        out_shape=jax.ShapeDtypeStruct((n, nh * HEAD_DIM), F32),
        scratch_shapes=[rows, rows, rows, rows, rows,
                        pltpu.VMEM((3 * len(halves), QBLK, 2 * QBLK), F32)],
        compiler_params=_params(("parallel", "arbitrary"), 60),
        name="dilated_attn",
    )(qkv_a, qkv_a, qkv_a)


NBR_UNROLL = 32


def _nbr_kernel(q_ref, k_ref, v_ref, bias_ref, o_ref, *, rows):
    scale = HEAD_DIM ** -0.5 * LOG2E
    wk = WIN_ROWS * GRID_W

    def scores(r):
        rs = jnp.clip(r - WIN_ROWS // 2, 0, rows - WIN_ROWS)
        q0 = pl.multiple_of(r * GRID_W, GRID_W)
        k0 = pl.multiple_of(rs * GRID_W, GRID_W)
        q = q_ref[pl.ds(q0, GRID_W), :]
        k = k_ref[pl.ds(k0, wk), :]
        s = lax.dot_general(q, k, (((1,), (1,)), ((), ())), preferred_element_type=F32) * scale
        return q0, k0, s + bias_ref[rs - r + (WIN_ROWS - 1)]

    def softmax(q0, k0, s):
        m = jnp.max(s, axis=-1, keepdims=True)
        p = jnp.exp2(s - m)
        return q0, k0, jnp.sum(p, axis=-1, keepdims=True), p.astype(BF16)

    def values(q0, k0, den, p):
        o = jnp.dot(p, v_ref[pl.ds(k0, wk), :], preferred_element_type=F32)
        o_ref[pl.ds(q0, GRID_W), :] = o / den

    unroll = NBR_UNROLL if rows % NBR_UNROLL == 0 else 1

    def trip(it, carry):
        parts = [scores(it * unroll + u) for u in range(unroll)]
        parts = [softmax(*part) for part in parts]
        for part in parts:
            values(*part)
        return carry

    lax.fori_loop(0, rows // unroll, trip, 0)


def _nbr_bias_tables(rel_bias):
    nh = rel_bias.shape[0]
    c = jnp.arange(GRID_W)
    cs = jnp.clip(c - WIN_COLS // 2, 0, GRID_W - WIN_COLS)
    col_mask = (c[None, :] >= cs[:, None]) & (c[None, :] < cs[:, None] + WIN_COLS)
    dc = jnp.clip(c[None, :] - c[:, None], -(WIN_COLS - 1), WIN_COLS - 1) + (WIN_COLS - 1)
    tab = rel_bias.astype(F32)[:, :, dc]
    tab = jnp.where(col_mask[None, None], tab * LOG2E, NEG_INF)
    idx = jnp.arange(WIN_ROWS)[:, None] + jnp.arange(WIN_ROWS)[None, :]
    tab = tab[:, idx]
    return tab.transpose(0, 1, 3, 2, 4).reshape(nh, WIN_ROWS, GRID_W, WIN_ROWS * GRID_W)


def _nbr_attention(qkvh, bias_tab, b, t, head0, n_heads):
    n = b * t
    rows = t // GRID_W
    assert t % GRID_W == 0 and rows >= WIN_ROWS
    nh = n_heads
    blk = (None, t, HEAD_DIM)
    kern = functools.partial(_nbr_kernel, rows=rows)
    return pl.pallas_call(
        kern,
        grid=(b, nh),
        in_specs=[pl.BlockSpec(blk, lambda bi, h: (head0 + h, bi, 0)),
                  pl.BlockSpec(blk, lambda bi, h: (head0 + nh + h, bi, 0)),
                  pl.BlockSpec(blk, lambda bi, h: (head0 + 2 * nh + h, bi, 0)),
                  pl.BlockSpec((None, WIN_ROWS, GRID_W, WIN_ROWS * GRID_W), lambda bi, h: (h, 0, 0, 0))],
        out_specs=pl.BlockSpec((t, HEAD_DIM), lambda bi, h: (bi, h)),
        out_shape=jax.ShapeDtypeStruct((n, nh * HEAD_DIM), F32),
        compiler_params=_params(("parallel", "arbitrary"), 48),
        name="nbr_attn",
    )(qkvh, qkvh, qkvh, bias_tab)


def _out_norms_kernel(oa_ref, ob_ref, ga_ref, gb_ref, out_ref):
    wa = oa_ref.shape[1]
    out_ref[:, :wa] = _rms(oa_ref[...], ga_ref[...]).astype(out_ref.dtype)
    out_ref[:, wa:] = _rms(ob_ref[...], gb_ref[...]).astype(out_ref.dtype)


def _out_norms(oa, ob, ga, gb):
    n, wa = oa.shape
    wb = ob.shape[1]
    bm = _pick(n, (256, 128, 8))
    return pl.pallas_call(
        _out_norms_kernel,
        grid=(n // bm,),
        in_specs=[pl.BlockSpec((bm, wa), lambda i: (i, 0)),
                  pl.BlockSpec((bm, wb), lambda i: (i, 0)),
                  pl.BlockSpec((1, wa), lambda i: (0, 0)),
                  pl.BlockSpec((1, wb), lambda i: (0, 0))],
        out_specs=pl.BlockSpec((bm, wa + wb), lambda i: (i, 0)),
        out_shape=jax.ShapeDtypeStruct((n, wa + wb), BF16),
        compiler_params=_params(("parallel",), 40),
        name="out_norms",
    )(oa, ob, ga.reshape(1, wa), gb.reshape(1, wb))


def _layer(x, b, t, wts, late, final_gain):
    (ffn1_norm, ffn1, mix_norm, w_in, bias_tab, out_norm_a, out_norm_b, w_out, ffn2_norm) = wts
    wa = out_norm_a.shape[0]
    wb = out_norm_b.shape[0]
    nha, nhb = wa // HEAD_DIM, wb // HEAD_DIM
    rope = _rope_tables(t)

    if 'bf16' in late:
        h1 = _ffn(x, ffn1_norm, *ffn1)
    else:
        stacks, layer = late['f32']
        n_steps = _ffn_steps(x.shape[0], ffn1[0].shape[1])[2]
        if all(_cast_block(*w.shape[1:], n_steps) for w in stacks):
            h1, late['bf16'] = _ffn(x, ffn1_norm, *ffn1, cast_next=[(w, layer) for w in stacks])
        else:
            h1, late['bf16'] = _ffn(x, ffn1_norm, *ffn1), tuple(_cast_bf16(w, layer) for w in stacks)
    ffn2 = late['bf16']
    u = _rmsnorm(h1, mix_norm, BF16)

    qkv_a = _qkv_proj(u, w_in, 0, 3 * wa, rope, t, 2 * wa, F32)
    qkv_b = _qkv_proj(u, w_in, 3 * wa, 3 * wb, rope, t, 0, BF16)
    oa = _dilated_attention(qkv_a, b, t, nha)
    ob = _nbr_attention(qkv_b, bias_tab, b, t, 0, nhb)
    merged = _out_norms(oa, ob, out_norm_a, out_norm_b)
    h2 = _out_proj(merged, w_out, h1)

    return _ffn(h2, ffn2_norm, *ffn2, g_out=final_gain)


def kernel(x_prompt, x_sample, ffn1_norm, ffn1_w_gate, ffn1_w_up, ffn1_w_down, mix_norm, w_in, nbr_rel_bias, out_norm_a, out_norm_b, w_out, ffn2_norm, ffn2_w_gate, ffn2_w_up, ffn2_w_down, final_norm):
    depth = ffn1_norm.shape[0]
    layers, late = [], []
    for i in range(depth):
        layers.append((ffn1_norm[i],
                       tuple(_cast_bf16(w, i) for w in (ffn1_w_gate, ffn1_w_up, ffn1_w_down)),
                       mix_norm[i], _cast_bf16(w_in, i), _nbr_bias_tables(nbr_rel_bias[i]),
                       out_norm_a[i], out_norm_b[i], _cast_bf16(w_out, i), ffn2_norm[i]))
        late.append({'f32': ((ffn2_w_gate, ffn2_w_up, ffn2_w_down), i)})

    def run(x3):
        b, t, d = x3.shape
        h = x3.reshape(b * t, d)
        for i in range(depth):
            h = _layer(h, b, t, layers[i], late[i], final_norm if i == depth - 1 else None)
        return h.reshape(b, t, d)

    return run(x_prompt), run(x_sample)
```

```python
import functools
import math

import jax
import jax.numpy as jnp
from jax import lax
from jax.experimental import pallas as pl
from jax.experimental.pallas import tpu as pltpu

HEAD_DIM = 128
DILATED_CONFIGS = ((128, 1), (512, 4), (2048, 16))
QBLK = 128
ROPE_THETA = 500000.0
ROT_DIM = HEAD_DIM // 4
GRID_W = 64
WIN_ROWS = 8
WIN_COLS = 16
NORM_EPS = 1e-6
NEG_INF = -1e30
LOG2E = math.log2(math.e)
LANES = 128
V7X_VMEM_BYTES = 64 * 1024 * 1024

F32 = jnp.float32
BF16 = jnp.bfloat16


def _params(sem, vmem_mb):
    return pltpu.CompilerParams(dimension_semantics=sem,
                                vmem_limit_bytes=min(vmem_mb * 1024 * 1024, V7X_VMEM_BYTES - (2 << 20)))


def _pick(n, prefs):
    for p in prefs:
        if n % p == 0:
            return p
    return n


def _cast_kernel(w_ref, o_ref):
    o_ref[...] = w_ref[...].astype(o_ref.dtype)


def _cast_bf16(w, layer):
    _, r, c = w.shape
    br = _pick(r, (256, 128, 16))
    return pl.pallas_call(
        _cast_kernel,
        grid=(r // br,),
        in_specs=[pl.BlockSpec((None, br, c), lambda i: (layer, i, 0))],
        out_specs=pl.BlockSpec((br, c), lambda i: (i, 0)),
        out_shape=jax.ShapeDtypeStruct((r, c), BF16),
        compiler_params=_params(("parallel",), 56),
        name="cast_bf16",
    )(w)


def _rms(x, g):
    ms = jnp.mean(x * x, axis=-1, keepdims=True)
    return x * lax.rsqrt(ms + NORM_EPS) * g


def _rmsnorm_kernel(x_ref, g_ref, o_ref):
    o_ref[...] = _rms(x_ref[...], g_ref[...]).astype(o_ref.dtype)


def _rmsnorm(x, g, out_dtype):
    n, d = x.shape
    bm = _pick(n, (256, 128, 8))
    return pl.pallas_call(
        _rmsnorm_kernel,
        grid=(n // bm,),
        in_specs=[pl.BlockSpec((bm, d), lambda i: (i, 0)),
                  pl.BlockSpec((1, d), lambda i: (0, 0))],
        out_specs=pl.BlockSpec((bm, d), lambda i: (i, 0)),
        out_shape=jax.ShapeDtypeStruct((n, d), out_dtype),
        compiler_params=_params(("parallel",), 40),
        name="rmsnorm",
    )(x, g.reshape(1, d))


FFN_ROW_CHUNK = 32


def _ffn_kernel(x_hbm, gin_ref, wg_ref, wu_ref, wd_ref, gout_ref, *rest, norm_out, n_cast):
    cast_in, (o_hbm, *cast_out) = rest[:n_cast], rest[n_cast:2 * n_cast + 1]
    acc, xn_scr, in_sem, out_sem = rest[2 * n_cast + 1:]
    i, j = pl.program_id(0), pl.program_id(1)
    n_tiles, n_chunks = pl.num_programs(0), pl.num_programs(1)
    _, n_rows, width = acc.shape
    slabs = [slice(c, c + LANES) for c in range(0, width, LANES)]
    slot = lax.rem(i, 2)
    tile = acc.at[slot]

    def x_copy(t, s):
        return pltpu.make_async_copy(x_hbm.at[pl.ds(t * n_rows, n_rows), :], acc.at[s], in_sem.at[s])

    def o_copy(t, s):
        return pltpu.make_async_copy(acc.at[s], o_hbm.at[pl.ds(t * n_rows, n_rows), :], out_sem.at[s])

    def row_loop(fn):
        def trip(c, carry):
            fn(pl.ds(pl.multiple_of(c * FFN_ROW_CHUNK, FFN_ROW_CHUNK), FFN_ROW_CHUNK))
            return carry
        lax.fori_loop(0, n_rows // FFN_ROW_CHUNK, trip, 0, unroll=2)

    def inv_rms(sq_sum):
        return lax.rsqrt(jnp.sum(sq_sum, axis=-1, keepdims=True) / width + NORM_EPS)

    @pl.when(j == 0)
    def _():
        @pl.when(i == 0)
        def _():
            x_copy(0, 0).start()
        x_copy(i, slot).wait()

        def prologue(rows):
            sq = jnp.zeros((FFN_ROW_CHUNK, LANES), F32)
            for sl in slabs:
                v = tile[rows, sl]
                sq = sq + v * v
            inv = inv_rms(sq)
            for sl in slabs:
                xn_scr[rows, sl] = (tile[rows, sl] * inv * gin_ref[:, sl]).astype(BF16)
        row_loop(prologue)

    xn = xn_scr[...]
    g = jnp.dot(xn, wg_ref[...], preferred_element_type=F32)
    u = jnp.dot(xn, wu_ref[...], preferred_element_type=F32)
    a = (0.5 * g * jax.nn.sigmoid(g) * u).astype(BF16)
    tile[...] += jnp.dot(a, wd_ref[...], preferred_element_type=F32)
    for src, dst in zip(cast_in, cast_out):
        dst[...] = src[...].astype(BF16)

    @pl.when(j == 1)
    def _():
        @pl.when(i >= 1)
        def _():
            o_copy(i - 1, 1 - slot).wait()

        @pl.when(i + 1 < n_tiles)
        def _():
            x_copy(i + 1, 1 - slot).start()

    @pl.when(j == n_chunks - 1)
    def _():
        if norm_out:
            def epilogue(rows):
                sq = jnp.zeros((FFN_ROW_CHUNK, LANES), F32)
                for sl in slabs:
                    h = tile[rows, sl]
                    sq = sq + h * h
                inv = inv_rms(sq)
                for sl in slabs:
                    tile[rows, sl] = tile[rows, sl] * inv * gout_ref[:, sl]
            row_loop(epilogue)
        o_copy(i, slot).start()

        @pl.when(i == n_tiles - 1)
        def _():
            o_copy(i, slot).wait()


CAST_BLOCKS = ((512, 256), (256, 512), (512, 512), (512, 1024))


def _cast_block(rows, cols, n_steps):
    for br, bc in CAST_BLOCKS:
        if rows % br == 0 and cols % bc == 0 and (rows // br) * (cols // bc) <= n_steps:
            return br, bc
    return None


def _ffn_steps(n, f):
    bm = _pick(n, (1024, 512, 256, 128, 32))
    bf = _pick(f, (256, 128))
    return bm, bf, (n // bm) * (f // bf)


def _ffn(x, g_in, wg, wu, wd, g_out=None, cast_next=()):
    n, d = x.shape
    f = wg.shape[1]
    bm, bf, n_steps = _ffn_steps(n, f)
    n_tiles, n_chunks = n // bm, f // bf
    assert f % bf == 0 and n_chunks >= 2 and bm % (2 * FFN_ROW_CHUNK) == 0
    norm_out = g_out is not None
    gain = pl.BlockSpec((1, d), lambda i, j: (0, 0))
    hbm = pl.BlockSpec(memory_space=pl.ANY)
    dma2 = pltpu.SemaphoreType.DMA((2,))
    cast_in_specs, cast_out_specs, cast_shapes, cast_args = [], [], [], []
    for w, layer in cast_next:
        _, rows, cols = w.shape
        blk = _cast_block(rows, cols, n_steps)
        assert blk is not None
        n_row, n_col = rows // blk[0], cols // blk[1]

        if n_col == n_chunks:
            def pos(i, j, n_row=n_row, n_col=n_col):
                return jnp.minimum(i, n_row - 1), jnp.where(i < n_row, j, n_col - 1)
        elif n_row == n_chunks:
            def pos(i, j, n_row=n_row, n_col=n_col):
                return jnp.where(i < n_col, j, n_row - 1), jnp.minimum(i, n_col - 1)
        else:
            def pos(i, j, n_row=n_row, n_col=n_col):
                unit = jnp.minimum(i * n_chunks + j, n_row * n_col - 1)
                return unit // n_col, unit % n_col

        cast_in_specs.append(pl.BlockSpec((None,) + blk, lambda i, j, pos=pos, layer=layer: (layer,) + pos(i, j)))
        cast_out_specs.append(pl.BlockSpec(blk, pos))
        cast_shapes.append(jax.ShapeDtypeStruct((rows, cols), BF16))
        cast_args.append(w)
    main = jax.ShapeDtypeStruct((n, d), F32)
    out = pl.pallas_call(
        functools.partial(_ffn_kernel, norm_out=norm_out, n_cast=len(cast_args)),
        grid=(n_tiles, n_chunks),
        in_specs=[hbm,
                  gain,
                  pl.BlockSpec((d, bf), lambda i, j: (0, j)),
                  pl.BlockSpec((d, bf), lambda i, j: (0, j)),
                  pl.BlockSpec((bf, d), lambda i, j: (j, 0)),
                  gain] + cast_in_specs,
        out_specs=[hbm] + cast_out_specs if cast_args else hbm,
        out_shape=[main] + cast_shapes if cast_args else main,
        scratch_shapes=[pltpu.VMEM((2, bm, d), F32), pltpu.VMEM((bm, d), BF16), dma2, dma2],
        compiler_params=_params(("arbitrary", "arbitrary"), 60),
        name="swiglu_ffn",
    )(x, g_in.reshape(1, d), wg, wu, wd, (g_out if norm_out else g_in).reshape(1, d), *cast_args)
    return (out[0], tuple(out[1:])) if cast_args else out


def _qkv_kernel(a_ref, w_ref, *refs, rotary, heads_per_tile):
    o_ref = refs[-1]
    acc = jnp.dot(a_ref[...], w_ref[...], preferred_element_type=F32)
    if rotary:
        c, sa, sb = (r[...] for r in refs[:3])
    for h in range(heads_per_tile):
        x = acc[:, h * HEAD_DIM:(h + 1) * HEAD_DIM]
        if rotary:
            x = (x * c + pltpu.roll(x, HEAD_DIM - ROT_DIM // 2, 1) * sa
                 + pltpu.roll(x, ROT_DIM // 2, 1) * sb)
        o_ref[h] = x.astype(o_ref.dtype)


def _qkv_proj(u, w, col0, m, rope, t, n_rope_cols, out_dtype):
    n, d = u.shape
    bm = _pick(t, (1024, 512, 256, 128))
    bn = next(p for p in (1024, 512, 256, 128)
              if m % p == 0 and n_rope_cols % p == 0 and col0 % p == 0)
    assert n % bm == 0
    tb = t // bm
    j0 = col0 // bn
    rope_tiles = n_rope_cols // bn
    rotary = rope_tiles > 0
    tab = pl.BlockSpec((None, bm, HEAD_DIM), lambda i, j: (jnp.where(j < rope_tiles, 0, 1), i % tb, 0))
    kern = functools.partial(_qkv_kernel, rotary=rotary, heads_per_tile=bn // HEAD_DIM)
    return pl.pallas_call(
        kern,
        grid=(n // bm, m // bn),
        in_specs=[pl.BlockSpec((bm, d), lambda i, j: (i, 0)),
                  pl.BlockSpec((d, bn), lambda i, j: (0, j0 + j))] + ([tab, tab, tab] if rotary else []),
        out_specs=pl.BlockSpec((bn // HEAD_DIM, bm, HEAD_DIM), lambda i, j: (j, i, 0)),
        out_shape=jax.ShapeDtypeStruct((m // HEAD_DIM, n, HEAD_DIM), out_dtype),
        compiler_params=_params(("parallel", "arbitrary"), 56),
        name="qkv_proj",
    )(u, w, *(rope if rotary else ()))


def _out_proj_kernel(a_ref, w_ref, r_ref, o_ref):
    o_ref[...] = r_ref[...] + jnp.dot(a_ref[...], w_ref[...], preferred_element_type=F32)


def _out_proj(a, w, res):
    n, k = a.shape
    m = w.shape[1]
    bm = _pick(n, (1024, 512, 256, 128, 8))
    bn = _pick(m, (1024, 512, 256, 128))
    return pl.pallas_call(
        _out_proj_kernel,
        grid=(n // bm, m // bn),
        in_specs=[pl.BlockSpec((bm, k), lambda i, j: (i, 0)),
                  pl.BlockSpec((k, bn), lambda i, j: (0, j)),
                  pl.BlockSpec((bm, bn), lambda i, j: (i, j))],
        out_specs=pl.BlockSpec((bm, bn), lambda i, j: (i, j)),
        out_shape=jax.ShapeDtypeStruct((n, m), F32),
        compiler_params=_params(("parallel", "arbitrary"), 56),
        name="out_proj",
    )(a, w, res)


def _rope_tables(t):
    pos = jnp.arange(t, dtype=F32)
    inv = ROPE_THETA ** (-jnp.arange(0, ROT_DIM, 2, dtype=F32) / ROT_DIM)
    ang = pos[:, None] * inv[None, :]
    cos, sin = jnp.cos(ang), jnp.sin(ang)
    half = ROT_DIM // 2
    z_half = jnp.zeros((t, half), F32)
    z_rest = jnp.zeros((t, HEAD_DIM - ROT_DIM), F32)
    c = jnp.concatenate([cos, cos, jnp.ones((t, HEAD_DIM - ROT_DIM), F32)], axis=1)
    sa = jnp.concatenate([-sin, z_half, z_rest], axis=1)
    sb = jnp.concatenate([z_half, sin, z_rest], axis=1)
    return (jnp.stack([c, jnp.ones_like(c)]), jnp.stack([sa, jnp.zeros_like(sa)]),
            jnp.stack([sb, jnp.zeros_like(sb)]))


DIL_UNROLL = 16


def _dilated_kernel(q_ref, k_ref, v_ref, o_ref, kp, vp, accp, mp, lp, mask_scr, *, t, p, halves):
    scale = HEAD_DIM ** -0.5 * LOG2E
    win = 2 * QBLK
    lgrp = t // p
    sub = QBLK // p
    log_p, log_sub = p.bit_length() - 1, sub.bit_length() - 1
    qi = lax.broadcasted_iota(jnp.int32, (QBLK, win), 0)
    ki = lax.broadcasted_iota(jnp.int32, (QBLK, win), 1)
    qi_regrouped = ((qi & (sub - 1)) << log_p) + (qi >> log_sub)

    for ci, half in enumerate(halves):
        rows_pos = qi_regrouped if ci == 2 else qi
        for vi, off in enumerate((0, half, QBLK)):
            mask_scr[3 * ci + vi] = jnp.where(jnp.abs(rows_pos - ki + off) <= half, 0.0, NEG_INF)

    def regroup(c, carry):
        r = lax.div(c, lgrp // QBLK)
        src = pl.ds(r + p * (c * QBLK - r * lgrp), QBLK, stride=p)
        dst = pl.ds(pl.multiple_of(c * QBLK, QBLK), QBLK)
        kp[dst, :] = k_ref[src, :]
        vp[dst, :] = v_ref[src, :]
        return carry

    lax.fori_loop(0, t // QBLK, regroup, 0, unroll=2)

    def softmax(blk, s):
        m = jnp.max(s, axis=-1, keepdims=True)
        e = jnp.exp2(s - m)
        return blk, m, jnp.sum(e, axis=-1, keepdims=True), e.astype(BF16)

    def run_pass(ci, d, load_q, load_kv, load_state, store_state):
        half = halves[ci]
        l = t // d
        nblk = l // QBLK
        shift = d.bit_length() - 1

        def scores(idx):
            i = lax.shift_right_logical(idx, shift)
            r = idx - (i << shift)
            q0 = i * QBLK
            k0 = jnp.clip(q0 - half, 0, l - win)
            placement = jnp.where(i == 0, 0, jnp.where(i == nblk - 1, 2, 1))
            q = load_q(r, q0).astype(BF16)
            k = load_kv(kp, k_ref, r, k0).astype(BF16)
            s = lax.dot_general(q, k, (((1,), (1,)), ((), ())), preferred_element_type=F32) * scale
            return (r, q0, k0), s + mask_scr[3 * ci + placement]

        def values(blk, m, den, e):
            r, _, k0 = blk
            v = load_kv(vp, v_ref, r, k0).astype(BF16)
            return blk, m, den, jnp.dot(e, v, preferred_element_type=F32)

        def merge(blk, m, den, acc):
            r, q0, _ = blk
            if ci == 0:
                store_state(r, q0, acc, jnp.broadcast_to(m, (QBLK, LANES)),
                            jnp.broadcast_to(den, (QBLK, LANES)))
                return
            acc_run, m_run, l_run = load_state(r, q0)
            m_new = jnp.maximum(m_run, m)
            a_run = jnp.exp2(m_run - m_new)
            a_blk = jnp.exp2(m - m_new)
            store_state(r, q0, acc_run * a_run + acc * a_blk, m_new, l_run * a_run + den * a_blk)

        total = d * nblk
        unroll = DIL_UNROLL if total % DIL_UNROLL == 0 else 1

        def trip(it, carry):
            parts = [scores(it * unroll + u) for u in range(unroll)]
            parts = [softmax(*part) for part in parts]
            parts = [values(*part) for part in parts]
            for part in parts:
                merge(*part)
            return carry

        lax.fori_loop(0, total // unroll, trip, 0)

    def store_regrouped(rows, acc, m, den):
        accp[rows, :] = acc
        mp[rows, :] = m
        lp[rows, :] = den

    def wide_rows(r, start, size):
        rr = lax.shift_right_logical(r, log_p)
        return pl.ds((r - (rr << log_p)) * lgrp + rr + p * start, size, stride=p)

    run_pass(0, p * p,
             load_q=lambda r, q0: q_ref[pl.ds(r + p * p * q0, QBLK, stride=p * p), :],
             load_kv=lambda grouped, natural, r, k0: grouped[wide_rows(r, k0, win), :],
             load_state=None,
             store_state=lambda r, q0, *state: store_regrouped(wide_rows(r, q0, QBLK), *state))

    def mid_rows(r, start, size):
        return pl.ds(pl.multiple_of(r * lgrp + start, 8), size)

    def mid_load(r, q0):
        rows = mid_rows(r, q0, QBLK)
        return accp[rows, :], mp[rows, :], lp[rows, :]

    run_pass(1, p,
             load_q=lambda r, q0: q_ref[pl.ds(r + p * q0, QBLK, stride=p), :],
             load_kv=lambda grouped, natural, r, k0: grouped[mid_rows(r, k0, win), :],
             load_state=mid_load,
             store_state=lambda r, q0, *state: store_regrouped(mid_rows(r, q0, QBLK), *state))

    def fine_load(r, q0):
        rows = [pl.ds(pl.multiple_of(c * lgrp + (q0 >> log_p), 8), sub) for c in range(p)]
        return tuple(jnp.concatenate([ref[rc, :] for rc in rows], axis=0) for ref in (accp, mp, lp))

    def fine_store(r, q0, acc, m, den):
        out = acc / den
        for c in range(p):
            o_ref[pl.ds(q0 + c, sub, stride=p), :] = out[c * sub:(c + 1) * sub]

    run_pass(2, 1,
             load_q=lambda r, q0: jnp.concatenate(
                 [q_ref[pl.ds(q0 + c, sub, stride=p), :] for c in range(p)], axis=0),
             load_kv=lambda grouped, natural, r, k0: natural[pl.ds(pl.multiple_of(k0, 8), win), :],
             load_state=fine_load, store_state=fine_store)


def _dilated_attention(qkv_a, b, t, n_heads):
    n = b * t
    nh = n_heads
    configs = sorted(((dil, window // (2 * dil)) for window, dil in DILATED_CONFIGS), reverse=True)
    (d_wide, _), (p, _), (d_fine, _) = configs
    halves = tuple(half for _, half in configs)
    assert d_fine == 1 and d_wide == p * p and p & (p - 1) == 0 and QBLK % (8 * p) == 0
    assert t % (d_wide * QBLK) == 0 and t // d_wide >= 2 * QBLK and max(halves) <= QBLK // 2
    blk = (None, t, HEAD_DIM)
    rows = pltpu.VMEM((t, LANES), F32)
    return pl.pallas_call(
        functools.partial(_dilated_kernel, t=t, p=p, halves=halves),
        grid=(b, nh),
        in_specs=[pl.BlockSpec(blk, lambda bi, h: (h, bi, 0)),
                  pl.BlockSpec(blk, lambda bi, h: (nh + h, bi, 0)),
                  pl.BlockSpec(blk, lambda bi, h: (2 * nh + h, bi, 0))],
        out_specs=pl.BlockSpec((t, HEAD_DIM), lambda bi, h: (bi, h)),
        out_shape=jax.ShapeDtypeStruct((n, nh * HEAD_DIM), F32),
        scratch_shapes=[rows, rows, rows, rows, rows,
                        pltpu.VMEM((3 * len(halves), QBLK, 2 * QBLK), F32)],
        compiler_params=_params(("parallel", "arbitrary"), 60),
        name="dilated_attn",
    )(qkv_a, qkv_a, qkv_a)


NBR_UNROLL = 32


def _nbr_kernel(q_ref, k_ref, v_ref, bias_ref, o_ref, *, rows):
    scale = HEAD_DIM ** -0.5 * LOG2E
    wk = WIN_ROWS * GRID_W

    def scores(r):
        rs = jnp.clip(r - WIN_ROWS // 2, 0, rows - WIN_ROWS)
        q0 = pl.multiple_of(r * GRID_W, GRID_W)
        k0 = pl.multiple_of(rs * GRID_W, GRID_W)
        q = q_ref[pl.ds(q0, GRID_W), :]
        k = k_ref[pl.ds(k0, wk), :]
        s = lax.dot_general(q, k, (((1,), (1,)), ((), ())), preferred_element_type=F32) * scale
        return q0, k0, s + bias_ref[rs - r + (WIN_ROWS - 1)]

    def softmax(q0, k0, s):
        m = jnp.max(s, axis=-1, keepdims=True)
        p = jnp.exp2(s - m)
        return q0, k0, jnp.sum(p, axis=-1, keepdims=True), p.astype(BF16)

    def values(q0, k0, den, p):
        o = jnp.dot(p, v_ref[pl.ds(k0, wk), :], preferred_element_type=F32)
        o_ref[pl.ds(q0, GRID_W), :] = o / den

    unroll = NBR_UNROLL if rows % NBR_UNROLL == 0 else 1

    def trip(it, carry):
        parts = [scores(it * unroll + u) for u in range(unroll)]
        parts = [softmax(*part) for part in parts]
        for part in parts:
            values(*part)
        return carry

    lax.fori_loop(0, rows // unroll, trip, 0)


def _nbr_bias_tables(rel_bias):
    key = jnp.arange(WIN_ROWS * GRID_W)
    key_row, key_col = key // GRID_W, key % GRID_W
    q_col = jnp.arange(GRID_W)
    cs = jnp.clip(q_col - WIN_COLS // 2, 0, GRID_W - WIN_COLS)
    col_mask = (key_col[None, :] >= cs[:, None]) & (key_col[None, :] < cs[:, None] + WIN_COLS)
    dc = jnp.clip(key_col[None, :] - q_col[:, None], -(WIN_COLS - 1), WIN_COLS - 1) + (WIN_COLS - 1)
    dr = jnp.arange(WIN_ROWS)[:, None] + key_row[None, :]
    tab = rel_bias.astype(F32)[:, dr[:, None, :], dc[None, :, :]]
    return jnp.where(col_mask[None, None], tab * LOG2E, NEG_INF)


def _nbr_attention(qkvh, bias_tab, b, t, head0, n_heads):
    n = b * t
    rows = t // GRID_W
    assert t % GRID_W == 0 and rows >= WIN_ROWS
    nh = n_heads
    blk = (None, t, HEAD_DIM)
    kern = functools.partial(_nbr_kernel, rows=rows)
    return pl.pallas_call(
        kern,
        grid=(b, nh),
        in_specs=[pl.BlockSpec(blk, lambda bi, h: (head0 + h, bi, 0)),
                  pl.BlockSpec(blk, lambda bi, h: (head0 + nh + h, bi, 0)),
                  pl.BlockSpec(blk, lambda bi, h: (head0 + 2 * nh + h, bi, 0)),
                  pl.BlockSpec((None, WIN_ROWS, GRID_W, WIN_ROWS * GRID_W), lambda bi, h: (h, 0, 0, 0))],
        out_specs=pl.BlockSpec((t, HEAD_DIM), lambda bi, h: (bi, h)),
        out_shape=jax.ShapeDtypeStruct((n, nh * HEAD_DIM), F32),
        compiler_params=_params(("parallel", "arbitrary"), 48),
        name="nbr_attn",
    )(qkvh, qkvh, qkvh, bias_tab)


def _out_norms_kernel(oa_ref, ob_ref, ga_ref, gb_ref, out_ref):
    wa = oa_ref.shape[1]
    out_ref[:, :wa] = _rms(oa_ref[...], ga_ref[...]).astype(out_ref.dtype)
    out_ref[:, wa:] = _rms(ob_ref[...], gb_ref[...]).astype(out_ref.dtype)


def _out_norms(oa, ob, ga, gb):
    n, wa = oa.shape
    wb = ob.shape[1]
    bm = _pick(n, (256, 128, 8))
    return pl.pallas_call(
        _out_norms_kernel,
        grid=(n // bm,),
        in_specs=[pl.BlockSpec((bm, wa), lambda i: (i, 0)),
                  pl.BlockSpec((bm, wb), lambda i: (i, 0)),
                  pl.BlockSpec((1, wa), lambda i: (0, 0)),
                  pl.BlockSpec((1, wb), lambda i: (0, 0))],
        out_specs=pl.BlockSpec((bm, wa + wb), lambda i: (i, 0)),
        out_shape=jax.ShapeDtypeStruct((n, wa + wb), BF16),
        compiler_params=_params(("parallel",), 40),
        name="out_norms",
    )(oa, ob, ga.reshape(1, wa), gb.reshape(1, wb))


def _layer(x, b, t, wts, late, final_gain):
    (ffn1_norm, ffn1, mix_norm, w_in, bias_tab, out_norm_a, out_norm_b, w_out, ffn2_norm) = wts
    wa = out_norm_a.shape[0]
    wb = out_norm_b.shape[0]
    nha, nhb = wa // HEAD_DIM, wb // HEAD_DIM
    rope = _rope_tables(t)

    if 'bf16' in late:
        h1 = _ffn(x, ffn1_norm, *ffn1)
    else:
        stacks, layer = late['f32']
        n_steps = _ffn_steps(x.shape[0], ffn1[0].shape[1])[2]
        if all(_cast_block(*w.shape[1:], n_steps) for w in stacks):
            h1, late['bf16'] = _ffn(x, ffn1_norm, *ffn1, cast_next=[(w, layer) for w in stacks])
        else:
            h1, late['bf16'] = _ffn(x, ffn1_norm, *ffn1), tuple(_cast_bf16(w, layer) for w in stacks)
    ffn2 = late['bf16']
    u = _rmsnorm(h1, mix_norm, BF16)

    qkv_a = _qkv_proj(u, w_in, 0, 3 * wa, rope, t, 2 * wa, F32)
    qkv_b = _qkv_proj(u, w_in, 3 * wa, 3 * wb, rope, t, 0, BF16)
    oa = _dilated_attention(qkv_a, b, t, nha)
    ob = _nbr_attention(qkv_b, bias_tab, b, t, 0, nhb)
    merged = _out_norms(oa, ob, out_norm_a, out_norm_b)
    h2 = _out_proj(merged, w_out, h1)

    return _ffn(h2, ffn2_norm, *ffn2, g_out=final_gain)


def kernel(x_prompt, x_sample, ffn1_norm, ffn1_w_gate, ffn1_w_up, ffn1_w_down, mix_norm, w_in, nbr_rel_bias, out_norm_a, out_norm_b, w_out, ffn2_norm, ffn2_w_gate, ffn2_w_up, ffn2_w_down, final_norm):
    depth = ffn1_norm.shape[0]
    layers, late = [], []
    for i in range(depth):
        layers.append((ffn1_norm[i],
                       tuple(_cast_bf16(w, i) for w in (ffn1_w_gate, ffn1_w_up, ffn1_w_down)),
                       mix_norm[i], _cast_bf16(w_in, i), _nbr_bias_tables(nbr_rel_bias[i]),
                       out_norm_a[i], out_norm_b[i], _cast_bf16(w_out, i), ffn2_norm[i]))
        late.append({'f32': ((ffn2_w_gate, ffn2_w_up, ffn2_w_down), i)})

    def run(x3):
        b, t, d = x3.shape
        h = x3.reshape(b * t, d)
        for i in range(depth):
            h = _layer(h, b, t, layers[i], late[i], final_norm if i == depth - 1 else None)
        return h.reshape(b, t, d)

    return run(x_prompt), run(x_sample)
```

```python
import functools
import math

import jax
import jax.numpy as jnp
from jax import lax
from jax.experimental import pallas as pl
from jax.experimental.pallas import tpu as pltpu

HEAD_DIM = 128
DILATED_CONFIGS = ((128, 1), (512, 4), (2048, 16))
QBLK = 128
ROPE_THETA = 500000.0
ROT_DIM = HEAD_DIM // 4
GRID_W = 64
WIN_ROWS = 8
WIN_COLS = 16
NORM_EPS = 1e-6
NEG_INF = -1e30
LOG2E = math.log2(math.e)
LANES = 128
V7X_VMEM_BYTES = 64 * 1024 * 1024

F32 = jnp.float32
BF16 = jnp.bfloat16


def _params(sem, vmem_mb):
    return pltpu.CompilerParams(dimension_semantics=sem,
                                vmem_limit_bytes=min(vmem_mb * 1024 * 1024, V7X_VMEM_BYTES - (2 << 20)))


def _pick(n, prefs):
    for p in prefs:
        if n % p == 0:
            return p
    return n


def _cast_kernel(w_ref, o_ref):
    o_ref[...] = w_ref[...].astype(o_ref.dtype)


def _cast_bf16(w, layer):
    _, r, c = w.shape
    br = _pick(r, (256, 128, 16))
    return pl.pallas_call(
        _cast_kernel,
        grid=(r // br,),
        in_specs=[pl.BlockSpec((None, br, c), lambda i: (layer, i, 0))],
        out_specs=pl.BlockSpec((br, c), lambda i: (i, 0)),
        out_shape=jax.ShapeDtypeStruct((r, c), BF16),
        compiler_params=_params(("parallel",), 56),
        name="cast_bf16",
    )(w)


def _rms(x, g):
    ms = jnp.mean(x * x, axis=-1, keepdims=True)
    return x * lax.rsqrt(ms + NORM_EPS) * g


def _rmsnorm_kernel(x_ref, g_ref, o_ref):
    o_ref[...] = _rms(x_ref[...], g_ref[...]).astype(o_ref.dtype)


def _rmsnorm(x, g, out_dtype):
    n, d = x.shape
    bm = _pick(n, (256, 128, 8))
    return pl.pallas_call(
        _rmsnorm_kernel,
        grid=(n // bm,),
        in_specs=[pl.BlockSpec((bm, d), lambda i: (i, 0)),
                  pl.BlockSpec((1, d), lambda i: (0, 0))],
        out_specs=pl.BlockSpec((bm, d), lambda i: (i, 0)),
        out_shape=jax.ShapeDtypeStruct((n, d), out_dtype),
        compiler_params=_params(("parallel",), 40),
        name="rmsnorm",
    )(x, g.reshape(1, d))


FFN_ROW_CHUNK = 32


def _ffn_kernel(x_hbm, gin_ref, wg_ref, wu_ref, wd_ref, gout_ref, *rest, norm_out, n_cast):
    cast_in, (o_hbm, *cast_out) = rest[:n_cast], rest[n_cast:2 * n_cast + 1]
    acc, xn_scr, in_sem, out_sem = rest[2 * n_cast + 1:]
    i, j = pl.program_id(0), pl.program_id(1)
    n_tiles, n_chunks = pl.num_programs(0), pl.num_programs(1)
    _, n_rows, width = acc.shape
    slabs = [slice(c, c + LANES) for c in range(0, width, LANES)]
    slot = lax.rem(i, 2)
    tile = acc.at[slot]

    def x_copy(t, s):
        return pltpu.make_async_copy(x_hbm.at[pl.ds(t * n_rows, n_rows), :], acc.at[s], in_sem.at[s])

    def o_copy(t, s):
        return pltpu.make_async_copy(acc.at[s], o_hbm.at[pl.ds(t * n_rows, n_rows), :], out_sem.at[s])

    def row_loop(fn):
        def trip(c, carry):
            fn(pl.ds(pl.multiple_of(c * FFN_ROW_CHUNK, FFN_ROW_CHUNK), FFN_ROW_CHUNK))
            return carry
        lax.fori_loop(0, n_rows // FFN_ROW_CHUNK, trip, 0, unroll=2)

    def inv_rms(sq_sum):
        return lax.rsqrt(jnp.sum(sq_sum, axis=-1, keepdims=True) / width + NORM_EPS)

    @pl.when(j == 0)
    def _():
        @pl.when(i == 0)
        def _():
            x_copy(0, 0).start()
        x_copy(i, slot).wait()

        def prologue(rows):
            sq = jnp.zeros((FFN_ROW_CHUNK, LANES), F32)
            for sl in slabs:
                v = tile[rows, sl]
                sq = sq + v * v
            inv = inv_rms(sq)
            for sl in slabs:
                xn_scr[rows, sl] = (tile[rows, sl] * inv * gin_ref[:, sl]).astype(BF16)
        row_loop(prologue)

    xn = xn_scr[...]
    g = jnp.dot(xn, wg_ref[...], preferred_element_type=F32)
    u = jnp.dot(xn, wu_ref[...], preferred_element_type=F32)
    a = (0.5 * g * jax.nn.sigmoid(g) * u).astype(BF16)
    tile[...] += jnp.dot(a, wd_ref[...], preferred_element_type=F32)
    for src, dst in zip(cast_in, cast_out):
        dst[...] = src[...].astype(BF16)

    @pl.when(j == 1)
    def _():
        @pl.when(i >= 1)
        def _():
            o_copy(i - 1, 1 - slot).wait()

        @pl.when(i + 1 < n_tiles)
        def _():
            x_copy(i + 1, 1 - slot).start()

    @pl.when(j == n_chunks - 1)
    def _():
        if norm_out:
            def epilogue(rows):
                sq = jnp.zeros((FFN_ROW_CHUNK, LANES), F32)
                for sl in slabs:
                    h = tile[rows, sl]
                    sq = sq + h * h
                inv = inv_rms(sq)
                for sl in slabs:
                    tile[rows, sl] = tile[rows, sl] * inv * gout_ref[:, sl]
            row_loop(epilogue)
        o_copy(i, slot).start()

        @pl.when(i == n_tiles - 1)
        def _():
            o_copy(i, slot).wait()


CAST_BLOCKS = ((512, 256), (256, 512), (512, 512), (512, 1024))


def _cast_block(rows, cols, n_steps):
    for br, bc in CAST_BLOCKS:
        if rows % br == 0 and cols % bc == 0 and (rows // br) * (cols // bc) <= n_steps:
            return br, bc
    return None


def _ffn_steps(n, f):
    bm = _pick(n, (1024, 512, 256, 128, 32))
    bf = _pick(f, (256, 128))
    return bm, bf, (n // bm) * (f // bf)


def _ffn(x, g_in, wg, wu, wd, g_out=None, cast_next=()):
    n, d = x.shape
    f = wg.shape[1]
    bm, bf, n_steps = _ffn_steps(n, f)
    n_tiles, n_chunks = n // bm, f // bf
    assert f % bf == 0 and n_chunks >= 2 and bm % (2 * FFN_ROW_CHUNK) == 0
    norm_out = g_out is not None
    gain = pl.BlockSpec((1, d), lambda i, j: (0, 0))
    hbm = pl.BlockSpec(memory_space=pl.ANY)
    dma2 = pltpu.SemaphoreType.DMA((2,))
    cast_in_specs, cast_out_specs, cast_shapes, cast_args = [], [], [], []
    for w, layer in cast_next:
        _, rows, cols = w.shape
        blk = _cast_block(rows, cols, n_steps)
        assert blk is not None
        n_row, n_col = rows // blk[0], cols // blk[1]

        if n_col == n_chunks:
            def pos(i, j, n_row=n_row, n_col=n_col):
                return jnp.minimum(i, n_row - 1), jnp.where(i < n_row, j, n_col - 1)
        elif n_row == n_chunks:
            def pos(i, j, n_row=n_row, n_col=n_col):
                return jnp.where(i < n_col, j, n_row - 1), jnp.minimum(i, n_col - 1)
        else:
            def pos(i, j, n_row=n_row, n_col=n_col):
                unit = jnp.minimum(i * n_chunks + j, n_row * n_col - 1)
                return unit // n_col, unit % n_col

        cast_in_specs.append(pl.BlockSpec((None,) + blk, lambda i, j, pos=pos, layer=layer: (layer,) + pos(i, j)))
        cast_out_specs.append(pl.BlockSpec(blk, pos))
        cast_shapes.append(jax.ShapeDtypeStruct((rows, cols), BF16))
        cast_args.append(w)
    main = jax.ShapeDtypeStruct((n, d), F32)
    out = pl.pallas_call(
        functools.partial(_ffn_kernel, norm_out=norm_out, n_cast=len(cast_args)),
        grid=(n_tiles, n_chunks),
        in_specs=[hbm,
                  gain,
                  pl.BlockSpec((d, bf), lambda i, j: (0, j)),
                  pl.BlockSpec((d, bf), lambda i, j: (0, j)),
                  pl.BlockSpec((bf, d), lambda i, j: (j, 0)),
                  gain] + cast_in_specs,
        out_specs=[hbm] + cast_out_specs if cast_args else hbm,
        out_shape=[main] + cast_shapes if cast_args else main,
        scratch_shapes=[pltpu.VMEM((2, bm, d), F32), pltpu.VMEM((bm, d), BF16), dma2, dma2],
        compiler_params=_params(("arbitrary", "arbitrary"), 60),
        name="swiglu_ffn",
    )(x, g_in.reshape(1, d), wg, wu, wd, (g_out if norm_out else g_in).reshape(1, d), *cast_args)
    return (out[0], tuple(out[1:])) if cast_args else out


def _qkv_kernel(a_ref, w_ref, *refs, rotary, heads_per_tile):
    o_ref = refs[-1]
    acc = jnp.dot(a_ref[...], w_ref[...], preferred_element_type=F32)
    if rotary:
        c, sa, sb = (r[...] for r in refs[:3])
    for h in range(heads_per_tile):
        x = acc[:, h * HEAD_DIM:(h + 1) * HEAD_DIM]
        if rotary:
            x = (x * c + pltpu.roll(x, HEAD_DIM - ROT_DIM // 2, 1) * sa
                 + pltpu.roll(x, ROT_DIM // 2, 1) * sb)
        o_ref[h] = x.astype(o_ref.dtype)


def _qkv_proj(u, w, col0, m, rope, t, n_rope_cols, out_dtype):
    n, d = u.shape
    bm = _pick(t, (1024, 512, 256, 128))
    bn = next(p for p in (1024, 512, 256, 128)
              if m % p == 0 and n_rope_cols % p == 0 and col0 % p == 0)
    assert n % bm == 0
    tb = t // bm
    j0 = col0 // bn
    rope_tiles = n_rope_cols // bn
    rotary = rope_tiles > 0
    tab = pl.BlockSpec((None, bm, HEAD_DIM), lambda i, j: (jnp.where(j < rope_tiles, 0, 1), i % tb, 0))
    kern = functools.partial(_qkv_kernel, rotary=rotary, heads_per_tile=bn // HEAD_DIM)
    return pl.pallas_call(
        kern,
        grid=(n // bm, m // bn),
        in_specs=[pl.BlockSpec((bm, d), lambda i, j: (i, 0)),
                  pl.BlockSpec((d, bn), lambda i, j: (0, j0 + j))] + ([tab, tab, tab] if rotary else []),
        out_specs=pl.BlockSpec((bn // HEAD_DIM, bm, HEAD_DIM), lambda i, j: (j, i, 0)),
        out_shape=jax.ShapeDtypeStruct((m // HEAD_DIM, n, HEAD_DIM), out_dtype),
        compiler_params=_params(("parallel", "arbitrary"), 56),
        name="qkv_proj",
    )(u, w, *(rope if rotary else ()))


def _out_proj_kernel(a_ref, w_ref, r_ref, o_ref):
    o_ref[...] = r_ref[...] + jnp.dot(a_ref[...], w_ref[...], preferred_element_type=F32)


def _out_proj(a, w, res):
    n, k = a.shape
    m = w.shape[1]
    bm = _pick(n, (1024, 512, 256, 128, 8))
    bn = _pick(m, (1024, 512, 256, 128))
    return pl.pallas_call(
        _out_proj_kernel,
        grid=(n // bm, m // bn),
        in_specs=[pl.BlockSpec((bm, k), lambda i, j: (i, 0)),
                  pl.BlockSpec((k, bn), lambda i, j: (0, j)),
                  pl.BlockSpec((bm, bn), lambda i, j: (i, j))],
        out_specs=pl.BlockSpec((bm, bn), lambda i, j: (i, j)),
        out_shape=jax.ShapeDtypeStruct((n, m), F32),
        compiler_params=_params(("parallel", "arbitrary"), 56),
        name="out_proj",
    )(a, w, res)


def _rope_tables(t):
    pos = jnp.arange(t, dtype=F32)
    inv = ROPE_THETA ** (-jnp.arange(0, ROT_DIM, 2, dtype=F32) / ROT_DIM)
    ang = pos[:, None] * inv[None, :]
    cos, sin = jnp.cos(ang), jnp.sin(ang)
    half = ROT_DIM // 2
    z_half = jnp.zeros((t, half), F32)
    z_rest = jnp.zeros((t, HEAD_DIM - ROT_DIM), F32)
    c = jnp.concatenate([cos, cos, jnp.ones((t, HEAD_DIM - ROT_DIM), F32)], axis=1)
    sa = jnp.concatenate([-sin, z_half, z_rest], axis=1)
    sb = jnp.concatenate([z_half, sin, z_rest], axis=1)
    return (jnp.stack([c, jnp.ones_like(c)]), jnp.stack([sa, jnp.zeros_like(sa)]),
            jnp.stack([sb, jnp.zeros_like(sb)]))


DIL_UNROLL = 16


def _dilated_kernel(q_ref, k_ref, v_ref, o_ref, kp, vp, accp, mp, lp, mask_scr, *, t, p, halves):
    scale = HEAD_DIM ** -0.5 * LOG2E
    win = 2 * QBLK
    lgrp = t // p
    sub = QBLK // p
    log_p, log_sub = p.bit_length() - 1, sub.bit_length() - 1
    qi = lax.broadcasted_iota(jnp.int32, (QBLK, win), 0)
    ki = lax.broadcasted_iota(jnp.int32, (QBLK, win), 1)
    qi_regrouped = ((qi & (sub - 1)) << log_p) + (qi >> log_sub)

    for ci, half in enumerate(halves):
        rows_pos = qi_regrouped if ci == 2 else qi
        for vi, off in enumerate((0, half, QBLK)):
            mask_scr[3 * ci + vi] = jnp.where(jnp.abs(rows_pos - ki + off) <= half, 0.0, NEG_INF)

    def regroup(c, carry):
        r = lax.div(c, lgrp // QBLK)
        src = pl.ds(r + p * (c * QBLK - r * lgrp), QBLK, stride=p)
        dst = pl.ds(pl.multiple_of(c * QBLK, QBLK), QBLK)
        kp[dst, :] = k_ref[src, :]
        vp[dst, :] = v_ref[src, :]
        return carry

    lax.fori_loop(0, t // QBLK, regroup, 0, unroll=2)

    def softmax(blk, s):
        m = jnp.max(s, axis=-1, keepdims=True)
        e = jnp.exp2(s - m)
        return blk, m, jnp.sum(e, axis=-1, keepdims=True), e.astype(BF16)

    def run_pass(ci, d, load_q, load_kv, load_state, store_state):
        half = halves[ci]
        l = t // d
        nblk = l // QBLK
        shift = d.bit_length() - 1

        def scores(idx):
            i = lax.shift_right_logical(idx, shift)
            r = idx - (i << shift)
            q0 = i * QBLK
            k0 = jnp.clip(q0 - half, 0, l - win)
            placement = jnp.where(i == 0, 0, jnp.where(i == nblk - 1, 2, 1))
            q = load_q(r, q0).astype(BF16)
            k = load_kv(kp, k_ref, r, k0).astype(BF16)
            s = lax.dot_general(q, k, (((1,), (1,)), ((), ())), preferred_element_type=F32) * scale
            return (r, q0, k0), s + mask_scr[3 * ci + placement]

        def values(blk, m, den, e):
            r, _, k0 = blk
            v = load_kv(vp, v_ref, r, k0).astype(BF16)
            return blk, m, den, jnp.dot(e, v, preferred_element_type=F32)

        def merge(blk, m, den, acc):
            r, q0, _ = blk
            if ci == 0:
                store_state(r, q0, acc, jnp.broadcast_to(m, (QBLK, LANES)),
                            jnp.broadcast_to(den, (QBLK, LANES)))
                return
            acc_run, m_run, l_run = load_state(r, q0)
            m_new = jnp.maximum(m_run, m)
            a_run = jnp.exp2(m_run - m_new)
            a_blk = jnp.exp2(m - m_new)
            store_state(r, q0, acc_run * a_run + acc * a_blk, m_new, l_run * a_run + den * a_blk)

        total = d * nblk
        unroll = DIL_UNROLL if total % DIL_UNROLL == 0 else 1

        def trip(it, carry):
            parts = [scores(it * unroll + u) for u in range(unroll)]
            parts = [softmax(*part) for part in parts]
            parts = [values(*part) for part in parts]
            for part in parts:
                merge(*part)
            return carry

        lax.fori_loop(0, total // unroll, trip, 0)

    def store_regrouped(rows, acc, m, den):
        accp[rows, :] = acc
        mp[rows, :] = m
        lp[rows, :] = den

    def wide_rows(r, start, size):
        rr = lax.shift_right_logical(r, log_p)
        return pl.ds((r - (rr << log_p)) * lgrp + rr + p * start, size, stride=p)

    run_pass(0, p * p,
             load_q=lambda r, q0: q_ref[pl.ds(r + p * p * q0, QBLK, stride=p * p), :],
             load_kv=lambda grouped, natural, r, k0: grouped[wide_rows(r, k0, win), :],
             load_state=None,
             store_state=lambda r, q0, *state: store_regrouped(wide_rows(r, q0, QBLK), *state))

    def mid_rows(r, start, size):
        return pl.ds(pl.multiple_of(r * lgrp + start, 8), size)

    def mid_load(r, q0):
        rows = mid_rows(r, q0, QBLK)
        return accp[rows, :], mp[rows, :], lp[rows, :]

    run_pass(1, p,
             load_q=lambda r, q0: q_ref[pl.ds(r + p * q0, QBLK, stride=p), :],
             load_kv=lambda grouped, natural, r, k0: grouped[mid_rows(r, k0, win), :],
             load_state=mid_load,
             store_state=lambda r, q0, *state: store_regrouped(mid_rows(r, q0, QBLK), *state))

    def fine_load(r, q0):
        rows = [pl.ds(pl.multiple_of(c * lgrp + (q0 >> log_p), 8), sub) for c in range(p)]
        return tuple(jnp.concatenate([ref[rc, :] for rc in rows], axis=0) for ref in (accp, mp, lp))

    def fine_store(r, q0, acc, m, den):
        out = acc / den
        for c in range(p):
            o_ref[pl.ds(q0 + c, sub, stride=p), :] = out[c * sub:(c + 1) * sub]

    run_pass(2, 1,
             load_q=lambda r, q0: jnp.concatenate(
                 [q_ref[pl.ds(q0 + c, sub, stride=p), :] for c in range(p)], axis=0),
             load_kv=lambda grouped, natural, r, k0: natural[pl.ds(pl.multiple_of(k0, 8), win), :],
             load_state=fine_load, store_state=fine_store)


def _dilated_attention(qkv_a, b, t, n_heads):
    n = b * t
    nh = n_heads
    configs = sorted(((dil, window // (2 * dil)) for window, dil in DILATED_CONFIGS), reverse=True)
    (d_wide, _), (p, _), (d_fine, _) = configs
    halves = tuple(half for _, half in configs)
    assert d_fine == 1 and d_wide == p * p and p & (p - 1) == 0 and QBLK % (8 * p) == 0
    assert t % (d_wide * QBLK) == 0 and t // d_wide >= 2 * QBLK and max(halves) <= QBLK // 2
    blk = (None, t, HEAD_DIM)
    rows = pltpu.VMEM((t, LANES), F32)
    return pl.pallas_call(
        functools.partial(_dilated_kernel, t=t, p=p, halves=halves),
        grid=(b, nh),
        in_specs=[pl.BlockSpec(blk, lambda bi, h: (h, bi, 0)),
                  pl.BlockSpec(blk, lambda bi, h: (nh + h, bi, 0)),
                  pl.BlockSpec(blk, lambda bi, h: (2 * nh + h, bi, 0))],
        out_specs=pl.BlockSpec((t, HEAD_DIM), lambda bi, h: (bi, h)),
        out_shape=jax.ShapeDtypeStruct((n, nh * HEAD_DIM), F32),
        scratch_shapes=[rows, rows, rows, rows, rows,
                        pltpu.VMEM((3 * len(halves), QBLK, 2 * QBLK), F32)],
        compiler_params=_params(("parallel", "arbitrary"), 60),
        name="dilated_attn",
    )(qkv_a, qkv_a, qkv_a)


NBR_UNROLL = 32


def _nbr_kernel(q_ref, k_ref, v_ref, bias_ref, o_ref, *, rows):
    scale = HEAD_DIM ** -0.5 * LOG2E
    wk = WIN_ROWS * GRID_W

    def scores(r):
        rs = jnp.clip(r - WIN_ROWS // 2, 0, rows - WIN_ROWS)
        q0 = pl.multiple_of(r * GRID_W, GRID_W)
        k0 = pl.multiple_of(rs * GRID_W, GRID_W)
        q = q_ref[pl.ds(q0, GRID_W), :]
        k = k_ref[pl.ds(k0, wk), :]
        s = lax.dot_general(q, k, (((1,), (1,)), ((), ())), preferred_element_type=F32) * scale
        return q0, k0, s + bias_ref[rs - r + (WIN_ROWS - 1)]

    def softmax(q0, k0, s):
        m = jnp.max(s, axis=-1, keepdims=True)
        p = jnp.exp2(s - m)
        return q0, k0, jnp.sum(p, axis=-1, keepdims=True), p.astype(BF16)

    def values(q0, k0, den, p):
        o = jnp.dot(p, v_ref[pl.ds(k0, wk), :], preferred_element_type=F32)
        o_ref[pl.ds(q0, GRID_W), :] = o / den

    unroll = NBR_UNROLL if rows % NBR_UNROLL == 0 else 1

    def trip(it, carry):
        parts = [scores(it * unroll + u) for u in range(unroll)]
        parts = [softmax(*part) for part in parts]
        for part in parts:
            values(*part)
        return carry

    lax.fori_loop(0, rows // unroll, trip, 0)


def _nbr_bias_tables(rel_bias):
    nh = rel_bias.shape[0]
    c = jnp.arange(GRID_W)
    cs = jnp.clip(c - WIN_COLS // 2, 0, GRID_W - WIN_COLS)
    col_mask = (c[None, :] >= cs[:, None]) & (c[None, :] < cs[:, None] + WIN_COLS)
    dc = jnp.clip(c[None, :] - c[:, None], -(WIN_COLS - 1), WIN_COLS - 1) + (WIN_COLS - 1)
    tab = rel_bias.astype(F32)[:, :, dc]
    tab = jnp.where(col_mask[None, None], tab * LOG2E, NEG_INF)
    idx = jnp.arange(WIN_ROWS)[:, None] + jnp.arange(WIN_ROWS)[None, :]
    tab = tab[:, idx]
    return tab.transpose(0, 1, 3, 2, 4).reshape(nh, WIN_ROWS, GRID_W, WIN_ROWS * GRID_W)


def _nbr_attention(qkvh, bias_tab, b, t, head0, n_heads):
    n = b * t
    rows = t // GRID_W
    assert t % GRID_W == 0 and rows >= WIN_ROWS
    nh = n_heads
    blk = (None, t, HEAD_DIM)
    kern = functools.partial(_nbr_kernel, rows=rows)
    return pl.pallas_call(
        kern,
        grid=(b, nh),
        in_specs=[pl.BlockSpec(blk, lambda bi, h: (head0 + h, bi, 0)),
                  pl.BlockSpec(blk, lambda bi, h: (head0 + nh + h, bi, 0)),
                  pl.BlockSpec(blk, lambda bi, h: (head0 + 2 * nh + h, bi, 0)),
                  pl.BlockSpec((None, WIN_ROWS, GRID_W, WIN_ROWS * GRID_W), lambda bi, h: (h, 0, 0, 0))],
        out_specs=pl.BlockSpec((t, HEAD_DIM), lambda bi, h: (bi, h)),
        out_shape=jax.ShapeDtypeStruct((n, nh * HEAD_DIM), F32),
        compiler_params=_params(("parallel", "arbitrary"), 48),
        name="nbr_attn",
    )(qkvh, qkvh, qkvh, bias_tab)


def _out_norms_kernel(oa_ref, ob_ref, ga_ref, gb_ref, out_ref):
    wa = oa_ref.shape[1]
    out_ref[:, :wa] = _rms(oa_ref[...], ga_ref[...]).astype(out_ref.dtype)
    out_ref[:, wa:] = _rms(ob_ref[...], gb_ref[...]).astype(out_ref.dtype)


def _out_norms(oa, ob, ga, gb):
    n, wa = oa.shape
    wb = ob.shape[1]
    bm = _pick(n, (256, 128, 8))
    return pl.pallas_call(
        _out_norms_kernel,
        grid=(n // bm,),
        in_specs=[pl.BlockSpec((bm, wa), lambda i: (i, 0)),
                  pl.BlockSpec((bm, wb), lambda i: (i, 0)),
                  pl.BlockSpec((1, wa), lambda i: (0, 0)),
                  pl.BlockSpec((1, wb), lambda i: (0, 0))],
        out_specs=pl.BlockSpec((bm, wa + wb), lambda i: (i, 0)),
        out_shape=jax.ShapeDtypeStruct((n, wa + wb), BF16),
        compiler_params=_params(("parallel",), 40),
        name="out_norms",
    )(oa, ob, ga.reshape(1, wa), gb.reshape(1, wb))


def _layer(x, b, t, wts, late, final_gain):
    (ffn1_norm, ffn1, mix_norm, w_in, bias_tab, out_norm_a, out_norm_b, w_out, ffn2_norm) = wts
    wa = out_norm_a.shape[0]
    wb = out_norm_b.shape[0]
    nha, nhb = wa // HEAD_DIM, wb // HEAD_DIM
    rope = _rope_tables(t)

    if 'bf16' in late:
        h1 = _ffn(x, ffn1_norm, *ffn1)
    else:
        stacks, layer = late['f32']
        n_steps = _ffn_steps(x.shape[0], ffn1[0].shape[1])[2]
        if all(_cast_block(*w.shape[1:], n_steps) for w in stacks):
            h1, late['bf16'] = _ffn(x, ffn1_norm, *ffn1, cast_next=[(w, layer) for w in stacks])
        else:
            h1, late['bf16'] = _ffn(x, ffn1_norm, *ffn1), tuple(_cast_bf16(w, layer) for w in stacks)
    ffn2 = late['bf16']
    u = _rmsnorm(h1, mix_norm, BF16)

    qkv_a = _qkv_proj(u, w_in, 0, 3 * wa, rope, t, 2 * wa, F32)
    qkv_b = _qkv_proj(u, w_in, 3 * wa, 3 * wb, rope, t, 0, BF16)
    oa = _dilated_attention(qkv_a, b, t, nha)
    ob = _nbr_attention(qkv_b, bias_tab, b, t, 0, nhb)
    merged = _out_norms(oa, ob, out_norm_a, out_norm_b)
    h2 = _out_proj(merged, w_out, h1)

    return _ffn(h2, ffn2_norm, *ffn2, g_out=final_gain)


def kernel(x_prompt, x_sample, ffn1_norm, ffn1_w_gate, ffn1_w_up, ffn1_w_down, mix_norm, w_in, nbr_rel_bias, out_norm_a, out_norm_b, w_out, ffn2_norm, ffn2_w_gate, ffn2_w_up, ffn2_w_down, final_norm):
    depth = ffn1_norm.shape[0]
    layers, late = [], []
    for i in range(depth):
        layers.append((ffn1_norm[i],
                       tuple(_cast_bf16(w, i) for w in (ffn1_w_gate, ffn1_w_up, ffn1_w_down)),
                       mix_norm[i], _cast_bf16(w_in, i), _nbr_bias_tables(nbr_rel_bias[i]),
                       out_norm_a[i], out_norm_b[i], _cast_bf16(w_out, i), ffn2_norm[i]))
        late.append({'f32': ((ffn2_w_gate, ffn2_w_up, ffn2_w_down), i)})

    def run(x3):
        b, t, d = x3.shape
        h = x3.reshape(b * t, d)
        for i in range(depth):
            h = _layer(h, b, t, layers[i], late[i], final_norm if i == depth - 1 else None)
        return h.reshape(b, t, d)

    return run(x_prompt), run(x_sample)
```

```python
import functools
import math

import jax
import jax.numpy as jnp
from jax import lax
from jax.experimental import pallas as pl
from jax.experimental.pallas import tpu as pltpu

HEAD_DIM = 128
DILATED_CONFIGS = ((128, 1), (512, 4), (2048, 16))
QBLK = 128
ROPE_THETA = 500000.0
ROT_DIM = HEAD_DIM // 4
GRID_W = 64
WIN_ROWS = 8
WIN_COLS = 16
NORM_EPS = 1e-6
NEG_INF = -1e30
LOG2E = math.log2(math.e)
LANES = 128
V7X_VMEM_BYTES = 64 * 1024 * 1024
VMEM_RESERVE_BYTES = 2 * 1024 * 1024

F32 = jnp.float32
BF16 = jnp.bfloat16


def _params(sem, vmem_mb):
    return pltpu.CompilerParams(dimension_semantics=sem,
                                vmem_limit_bytes=min(vmem_mb * 1024 * 1024,
                                                     V7X_VMEM_BYTES - VMEM_RESERVE_BYTES))


def _pick(n, prefs):
    for p in prefs:
        if n % p == 0:
            return p
    return n


def _cast_kernel(w_ref, o_ref):
    o_ref[...] = w_ref[...].astype(o_ref.dtype)


def _cast_bf16(w, layer):
    _, r, c = w.shape
    br = _pick(r, (256, 128, 16))
    return pl.pallas_call(
        _cast_kernel,
        grid=(r // br,),
        in_specs=[pl.BlockSpec((None, br, c), lambda i: (layer, i, 0))],
        out_specs=pl.BlockSpec((br, c), lambda i: (i, 0)),
        out_shape=jax.ShapeDtypeStruct((r, c), BF16),
        compiler_params=_params(("parallel",), 56),
        name="cast_bf16",
    )(w)


def _rms(x, g):
    ms = jnp.mean(x * x, axis=-1, keepdims=True)
    return x * lax.rsqrt(ms + NORM_EPS) * g


def _rmsnorm_kernel(x_ref, g_ref, o_ref):
    o_ref[...] = _rms(x_ref[...], g_ref[...]).astype(o_ref.dtype)


def _rmsnorm(x, g, out_dtype):
    n, d = x.shape
    bm = _pick(n, (512, 256, 128, 8))
    return pl.pallas_call(
        _rmsnorm_kernel,
        grid=(n // bm,),
        in_specs=[pl.BlockSpec((bm, d), lambda i: (i, 0)),
                  pl.BlockSpec((1, d), lambda i: (0, 0))],
        out_specs=pl.BlockSpec((bm, d), lambda i: (i, 0)),
        out_shape=jax.ShapeDtypeStruct((n, d), out_dtype),
        compiler_params=_params(("parallel",), 40),
        name="rmsnorm",
    )(x, g.reshape(1, d))


FFN_ROW_CHUNK = 32


def _ffn_kernel(x_hbm, gin_ref, wg_ref, wu_ref, wd_ref, gout_ref, *rest, norm_out, n_cast):
    cast_in, (o_hbm, *cast_out) = rest[:n_cast], rest[n_cast:2 * n_cast + 1]
    acc, xn_scr, in_sem, out_sem = rest[2 * n_cast + 1:]
    i, j = pl.program_id(0), pl.program_id(1)
    n_tiles, n_chunks = pl.num_programs(0), pl.num_programs(1)
    _, n_rows, width = acc.shape
    slabs = [slice(c, c + LANES) for c in range(0, width, LANES)]
    slot = lax.rem(i, 2)
    tile = acc.at[slot]

    def x_copy(t, s):
        return pltpu.make_async_copy(x_hbm.at[pl.ds(t * n_rows, n_rows), :], acc.at[s], in_sem.at[s])

    def o_copy(t, s):
        return pltpu.make_async_copy(acc.at[s], o_hbm.at[pl.ds(t * n_rows, n_rows), :], out_sem.at[s])

    def row_loop(fn):
        def trip(c, carry):
            fn(pl.ds(pl.multiple_of(c * FFN_ROW_CHUNK, FFN_ROW_CHUNK), FFN_ROW_CHUNK))
            return carry
        lax.fori_loop(0, n_rows // FFN_ROW_CHUNK, trip, 0, unroll=2)

    def inv_rms(sq_sum):
        return lax.rsqrt(jnp.sum(sq_sum, axis=-1, keepdims=True) / width + NORM_EPS)

    @pl.when(j == 0)
    def _():
        @pl.when(i == 0)
        def _():
            x_copy(0, 0).start()
        x_copy(i, slot).wait()

        def prologue(rows):
            sq = jnp.zeros((FFN_ROW_CHUNK, LANES), F32)
            for sl in slabs:
                v = tile[rows, sl]
                sq = sq + v * v
            inv = inv_rms(sq)
            for sl in slabs:
                xn_scr[rows, sl] = (tile[rows, sl] * inv * gin_ref[:, sl]).astype(BF16)
        row_loop(prologue)

    xn = xn_scr[...]
    g = jnp.dot(xn, wg_ref[...], preferred_element_type=F32)
    u = jnp.dot(xn, wu_ref[...], preferred_element_type=F32)
    a = (0.5 * g * jax.nn.sigmoid(g) * u).astype(BF16)
    tile[...] += jnp.dot(a, wd_ref[...], preferred_element_type=F32)
    for src, dst in zip(cast_in, cast_out):
        dst[...] = src[...].astype(BF16)

    @pl.when(j == 1)
    def _():
        @pl.when(i >= 1)
        def _():
            o_copy(i - 1, 1 - slot).wait()

        @pl.when(i + 1 < n_tiles)
        def _():
            x_copy(i + 1, 1 - slot).start()

    @pl.when(j == n_chunks - 1)
    def _():
        if norm_out:
            def epilogue(rows):
                sq = jnp.zeros((FFN_ROW_CHUNK, LANES), F32)
                for sl in slabs:
                    h = tile[rows, sl]
                    sq = sq + h * h
                inv = inv_rms(sq)
                for sl in slabs:
                    tile[rows, sl] = tile[rows, sl] * inv * gout_ref[:, sl]
            row_loop(epilogue)
        o_copy(i, slot).start()

        @pl.when(i == n_tiles - 1)
        def _():
            o_copy(i, slot).wait()


CAST_BLOCKS = ((512, 256), (256, 512), (512, 512), (512, 1024))


def _cast_block(rows, cols, n_steps):
    for br, bc in CAST_BLOCKS:
        if rows % br == 0 and cols % bc == 0 and (rows // br) * (cols // bc) <= n_steps:
            return br, bc
    return None


def _ffn_steps(n, f):
    bm = _pick(n, (1024, 512, 256, 128, 32))
    bf = _pick(f, (256, 128))
    return bm, bf, (n // bm) * (f // bf)


def _ffn(x, g_in, wg, wu, wd, g_out=None, cast_next=()):
    n, d = x.shape
    f = wg.shape[1]
    bm, bf, n_steps = _ffn_steps(n, f)
    n_tiles, n_chunks = n // bm, f // bf
    assert f % bf == 0 and n_chunks >= 2 and bm % (2 * FFN_ROW_CHUNK) == 0
    norm_out = g_out is not None
    gain = pl.BlockSpec((1, d), lambda i, j: (0, 0))
    hbm = pl.BlockSpec(memory_space=pl.ANY)
    dma2 = pltpu.SemaphoreType.DMA((2,))
    cast_in_specs, cast_out_specs, cast_shapes, cast_args = [], [], [], []
    for w, layer in cast_next:
        _, rows, cols = w.shape
        blk = _cast_block(rows, cols, n_steps)
        assert blk is not None
        n_row, n_col = rows // blk[0], cols // blk[1]

        if n_col == n_chunks:
            def pos(i, j, n_row=n_row, n_col=n_col):
                return jnp.minimum(i, n_row - 1), jnp.where(i < n_row, j, n_col - 1)
        elif n_row == n_chunks:
            def pos(i, j, n_row=n_row, n_col=n_col):
                return jnp.where(i < n_col, j, n_row - 1), jnp.minimum(i, n_col - 1)
        else:
            def pos(i, j, n_row=n_row, n_col=n_col):
                unit = jnp.minimum(i * n_chunks + j, n_row * n_col - 1)
                return unit // n_col, unit % n_col

        cast_in_specs.append(pl.BlockSpec((None,) + blk, lambda i, j, pos=pos, layer=layer: (layer,) + pos(i, j)))
        cast_out_specs.append(pl.BlockSpec(blk, pos))
        cast_shapes.append(jax.ShapeDtypeStruct((rows, cols), BF16))
        cast_args.append(w)
    main = jax.ShapeDtypeStruct((n, d), F32)
    out = pl.pallas_call(
        functools.partial(_ffn_kernel, norm_out=norm_out, n_cast=len(cast_args)),
        grid=(n_tiles, n_chunks),
        in_specs=[hbm,
                  gain,
                  pl.BlockSpec((d, bf), lambda i, j: (0, j)),
                  pl.BlockSpec((d, bf), lambda i, j: (0, j)),
                  pl.BlockSpec((bf, d), lambda i, j: (j, 0)),
                  gain] + cast_in_specs,
        out_specs=[hbm] + cast_out_specs if cast_args else hbm,
        out_shape=[main] + cast_shapes if cast_args else main,
        scratch_shapes=[pltpu.VMEM((2, bm, d), F32), pltpu.VMEM((bm, d), BF16), dma2, dma2],
        compiler_params=_params(("arbitrary", "arbitrary"), 60),
        name="swiglu_ffn",
    )(x, g_in.reshape(1, d), wg, wu, wd, (g_out if norm_out else g_in).reshape(1, d), *cast_args)
    return (out[0], tuple(out[1:])) if cast_args else out


def _qkv_kernel(a_ref, w_ref, *refs, rotary, heads_per_tile):
    o_ref = refs[-1]
    acc = jnp.dot(a_ref[...], w_ref[...], preferred_element_type=F32)
    if rotary:
        c, sa, sb = (r[...] for r in refs[:3])
    for h in range(heads_per_tile):
        x = acc[:, h * HEAD_DIM:(h + 1) * HEAD_DIM]
        if rotary:
            x = (x * c + pltpu.roll(x, HEAD_DIM - ROT_DIM // 2, 1) * sa
                 + pltpu.roll(x, ROT_DIM // 2, 1) * sb)
        o_ref[h] = x.astype(o_ref.dtype)


def _qkv_proj(u, w, col0, m, rope, t, n_rope_cols, out_dtype):
    n, d = u.shape
    bm = _pick(t, (1024, 512, 256, 128))
    bn = next(p for p in (1024, 512, 256, 128)
              if m % p == 0 and n_rope_cols % p == 0 and col0 % p == 0)
    assert n % bm == 0
    tb = t // bm
    j0 = col0 // bn
    rope_tiles = n_rope_cols // bn
    rotary = rope_tiles > 0
    tab = pl.BlockSpec((None, bm, HEAD_DIM), lambda i, j: (jnp.where(j < rope_tiles, 0, 1), i % tb, 0))
    kern = functools.partial(_qkv_kernel, rotary=rotary, heads_per_tile=bn // HEAD_DIM)
    return pl.pallas_call(
        kern,
        grid=(n // bm, m // bn),
        in_specs=[pl.BlockSpec((bm, d), lambda i, j: (i, 0)),
                  pl.BlockSpec((d, bn), lambda i, j: (0, j0 + j))] + ([tab, tab, tab] if rotary else []),
        out_specs=pl.BlockSpec((bn // HEAD_DIM, bm, HEAD_DIM), lambda i, j: (j, i, 0)),
        out_shape=jax.ShapeDtypeStruct((m // HEAD_DIM, n, HEAD_DIM), out_dtype),
        compiler_params=_params(("parallel", "arbitrary"), 56),
        name="qkv_proj",
    )(u, w, *(rope if rotary else ()))


def _out_proj_kernel(a_ref, w_ref, r_ref, o_ref):
    o_ref[...] = r_ref[...] + jnp.dot(a_ref[...], w_ref[...], preferred_element_type=F32)


def _out_proj(a, w, res):
    n, k = a.shape
    m = w.shape[1]
    bm = _pick(n, (1024, 512, 256, 128, 8))
    bn = _pick(m, (1024, 512, 256, 128))
    return pl.pallas_call(
        _out_proj_kernel,
        grid=(n // bm, m // bn),
        in_specs=[pl.BlockSpec((bm, k), lambda i, j: (i, 0)),
                  pl.BlockSpec((k, bn), lambda i, j: (0, j)),
                  pl.BlockSpec((bm, bn), lambda i, j: (i, j))],
        out_specs=pl.BlockSpec((bm, bn), lambda i, j: (i, j)),
        out_shape=jax.ShapeDtypeStruct((n, m), F32),
        compiler_params=_params(("parallel", "arbitrary"), 56),
        name="out_proj",
    )(a, w, res)


def _rope_tables(t):
    pos = jnp.arange(t, dtype=F32)
    inv = ROPE_THETA ** (-jnp.arange(0, ROT_DIM, 2, dtype=F32) / ROT_DIM)
    ang = pos[:, None] * inv[None, :]
    cos, sin = jnp.cos(ang), jnp.sin(ang)
    half = ROT_DIM // 2
    z_half = jnp.zeros((t, half), F32)
    z_rest = jnp.zeros((t, HEAD_DIM - ROT_DIM), F32)
    c = jnp.concatenate([cos, cos, jnp.ones((t, HEAD_DIM - ROT_DIM), F32)], axis=1)
    sa = jnp.concatenate([-sin, z_half, z_rest], axis=1)
    sb = jnp.concatenate([z_half, sin, z_rest], axis=1)
    return (jnp.stack([c, jnp.ones_like(c)]), jnp.stack([sa, jnp.zeros_like(sa)]),
            jnp.stack([sb, jnp.zeros_like(sb)]))


DIL_UNROLL = 16


def _dilated_kernel(q_ref, k_ref, v_ref, o_ref, kp, vp, accp, mp, lp, mask_scr, *, t, p, halves):
    scale = HEAD_DIM ** -0.5 * LOG2E
    win = 2 * QBLK
    lgrp = t // p
    sub = QBLK // p
    log_p, log_sub = p.bit_length() - 1, sub.bit_length() - 1
    qi = lax.broadcasted_iota(jnp.int32, (QBLK, win), 0)
    ki = lax.broadcasted_iota(jnp.int32, (QBLK, win), 1)
    qi_regrouped = ((qi & (sub - 1)) << log_p) + (qi >> log_sub)

    for ci, half in enumerate(halves):
        rows_pos = qi_regrouped if ci == 2 else qi
        for vi, off in enumerate((0, half, QBLK)):
            mask_scr[3 * ci + vi] = jnp.where(jnp.abs(rows_pos - ki + off) <= half, 0.0, NEG_INF)

    def regroup(c, carry):
        r = lax.div(c, lgrp // QBLK)
        src = pl.ds(r + p * (c * QBLK - r * lgrp), QBLK, stride=p)
        dst = pl.ds(pl.multiple_of(c * QBLK, QBLK), QBLK)
        kp[dst, :] = k_ref[src, :]
        vp[dst, :] = v_ref[src, :]
        return carry

    lax.fori_loop(0, t // QBLK, regroup, 0, unroll=2)

    def softmax(blk, s):
        m = jnp.max(s, axis=-1, keepdims=True)
        e = jnp.exp2(s - m)
        return blk, m, jnp.sum(e, axis=-1, keepdims=True), e.astype(BF16)

    def run_pass(ci, d, load_q, load_kv, load_state, store_state):
        half = halves[ci]
        l = t // d
        nblk = l // QBLK
        shift = d.bit_length() - 1

        def scores(idx):
            i = lax.shift_right_logical(idx, shift)
            r = idx - (i << shift)
            q0 = i * QBLK
            k0 = jnp.clip(q0 - half, 0, l - win)
            placement = jnp.where(i == 0, 0, jnp.where(i == nblk - 1, 2, 1))
            q = load_q(r, q0).astype(BF16)
            k = load_kv(kp, k_ref, r, k0).astype(BF16)
            s = lax.dot_general(q, k, (((1,), (1,)), ((), ())), preferred_element_type=F32) * scale
            return (r, q0, k0), s + mask_scr[3 * ci + placement]

        def values(blk, m, den, e):
            r, _, k0 = blk
            v = load_kv(vp, v_ref, r, k0).astype(BF16)
            return blk, m, den, jnp.dot(e, v, preferred_element_type=F32)

        def merge(blk, m, den, acc):
            r, q0, _ = blk
            if ci == 0:
                store_state(r, q0, acc, jnp.broadcast_to(m, (QBLK, LANES)),
                            jnp.broadcast_to(den, (QBLK, LANES)))
                return
            acc_run, m_run, l_run = load_state(r, q0)
            m_new = jnp.maximum(m_run, m)
            a_run = jnp.exp2(m_run - m_new)
            a_blk = jnp.exp2(m - m_new)
            store_state(r, q0, acc_run * a_run + acc * a_blk, m_new, l_run * a_run + den * a_blk)

        total = d * nblk
        unroll = DIL_UNROLL if total % DIL_UNROLL == 0 else 1

        def trip(it, carry):
            parts = [scores(it * unroll + u) for u in range(unroll)]
            parts = [softmax(*part) for part in parts]
            parts = [values(*part) for part in parts]
            for part in parts:
                merge(*part)
            return carry

        lax.fori_loop(0, total // unroll, trip, 0)

    def store_regrouped(rows, acc, m, den):
        accp[rows, :] = acc
        mp[rows, :] = m
        lp[rows, :] = den

    def wide_rows(r, start, size):
        rr = lax.shift_right_logical(r, log_p)
        return pl.ds((r - (rr << log_p)) * lgrp + rr + p * start, size, stride=p)

    run_pass(0, p * p,
             load_q=lambda r, q0: q_ref[pl.ds(r + p * p * q0, QBLK, stride=p * p), :],
             load_kv=lambda grouped, natural, r, k0: grouped[wide_rows(r, k0, win), :],
             load_state=None,
             store_state=lambda r, q0, *state: store_regrouped(wide_rows(r, q0, QBLK), *state))

    def mid_rows(r, start, size):
        return pl.ds(pl.multiple_of(r * lgrp + start, 8), size)

    def mid_load(r, q0):
        rows = mid_rows(r, q0, QBLK)
        return accp[rows, :], mp[rows, :], lp[rows, :]

    run_pass(1, p,
             load_q=lambda r, q0: q_ref[pl.ds(r + p * q0, QBLK, stride=p), :],
             load_kv=lambda grouped, natural, r, k0: grouped[mid_rows(r, k0, win), :],
             load_state=mid_load,
             store_state=lambda r, q0, *state: store_regrouped(mid_rows(r, q0, QBLK), *state))

    def fine_load(r, q0):
        rows = [pl.ds(pl.multiple_of(c * lgrp + (q0 >> log_p), 8), sub) for c in range(p)]
        return tuple(jnp.concatenate([ref[rc, :] for rc in rows], axis=0) for ref in (accp, mp, lp))

    def fine_store(r, q0, acc, m, den):
        out = acc / den
        for c in range(p):
            o_ref[pl.ds(q0 + c, sub, stride=p), :] = out[c * sub:(c + 1) * sub]

    run_pass(2, 1,
             load_q=lambda r, q0: jnp.concatenate(
                 [q_ref[pl.ds(q0 + c, sub, stride=p), :] for c in range(p)], axis=0),
             load_kv=lambda grouped, natural, r, k0: natural[pl.ds(pl.multiple_of(k0, 8), win), :],
             load_state=fine_load, store_state=fine_store)


def _dilated_attention(qkv_a, b, t, n_heads):
    n = b * t
    nh = n_heads
    configs = sorted(((dil, window // (2 * dil)) for window, dil in DILATED_CONFIGS), reverse=True)
    (d_wide, _), (p, _), (d_fine, _) = configs
    halves = tuple(half for _, half in configs)
    assert d_fine == 1 and d_wide == p * p and p & (p - 1) == 0 and QBLK % (8 * p) == 0
    assert t % (d_wide * QBLK) == 0 and t // d_wide >= 2 * QBLK and max(halves) <= QBLK // 2
    blk = (None, t, HEAD_DIM)
    rows = pltpu.VMEM((t, LANES), F32)
    return pl.pallas_call(
        functools.partial(_dilated_kernel, t=t, p=p, halves=halves),
        grid=(b, nh),
        in_specs=[pl.BlockSpec(blk, lambda bi, h: (h, bi, 0)),
                  pl.BlockSpec(blk, lambda bi, h: (nh + h, bi, 0)),
                  pl.BlockSpec(blk, lambda bi, h: (2 * nh + h, bi, 0))],
        out_specs=pl.BlockSpec((t, HEAD_DIM), lambda bi, h: (bi, h)),
        out_shape=jax.ShapeDtypeStruct((n, nh * HEAD_DIM), F32),
        scratch_shapes=[rows, rows, rows, rows, rows,
                        pltpu.VMEM((3 * len(halves), QBLK, 2 * QBLK), F32)],
        compiler_params=_params(("parallel", "arbitrary"), 60),
        name="dilated_attn",
    )(qkv_a, qkv_a, qkv_a)


NBR_UNROLL = 32


def _nbr_kernel(q_ref, k_ref, v_ref, bias_ref, o_ref, *, rows):
    scale = HEAD_DIM ** -0.5 * LOG2E
    wk = WIN_ROWS * GRID_W

    def scores(r):
        rs = jnp.clip(r - WIN_ROWS // 2, 0, rows - WIN_ROWS)
        q0 = pl.multiple_of(r * GRID_W, GRID_W)
        k0 = pl.multiple_of(rs * GRID_W, GRID_W)
        q = q_ref[pl.ds(q0, GRID_W), :]
        k = k_ref[pl.ds(k0, wk), :]
        s = lax.dot_general(q, k, (((1,), (1,)), ((), ())), preferred_element_type=F32) * scale
        return q0, k0, s + bias_ref[rs - r + (WIN_ROWS - 1)]

    def softmax(q0, k0, s):
        m = jnp.max(s, axis=-1, keepdims=True)
        p = jnp.exp2(s - m)
        return q0, k0, jnp.sum(p, axis=-1, keepdims=True), p.astype(BF16)

    def values(q0, k0, den, p):
        o = jnp.dot(p, v_ref[pl.ds(k0, wk), :], preferred_element_type=F32)
        o_ref[pl.ds(q0, GRID_W), :] = o / den

    unroll = NBR_UNROLL if rows % NBR_UNROLL == 0 else 1

    def trip(it, carry):
        parts = [scores(it * unroll + u) for u in range(unroll)]
        parts = [softmax(*part) for part in parts]
        for part in parts:
            values(*part)
        return carry

    lax.fori_loop(0, rows // unroll, trip, 0)


def _nbr_bias_tables(rel_bias):
    nh = rel_bias.shape[0]
    c = jnp.arange(GRID_W)
    cs = jnp.clip(c - WIN_COLS // 2, 0, GRID_W - WIN_COLS)
    col_mask = (c[None, :] >= cs[:, None]) & (c[None, :] < cs[:, None] + WIN_COLS)
    dc = jnp.clip(c[None, :] - c[:, None], -(WIN_COLS - 1), WIN_COLS - 1) + (WIN_COLS - 1)
    tab = rel_bias.astype(F32)[:, :, dc]
    tab = jnp.where(col_mask[None, None], tab * LOG2E, NEG_INF)
    idx = jnp.arange(WIN_ROWS)[:, None] + jnp.arange(WIN_ROWS)[None, :]
    tab = tab[:, idx]
    return tab.transpose(0, 1, 3, 2, 4).reshape(nh, WIN_ROWS, GRID_W, WIN_ROWS * GRID_W)


def _nbr_attention(qkvh, bias_tab, b, t, head0, n_heads):
    n = b * t
    rows = t // GRID_W
    assert t % GRID_W == 0 and rows >= WIN_ROWS
    nh = n_heads
    blk = (None, t, HEAD_DIM)
    kern = functools.partial(_nbr_kernel, rows=rows)
    return pl.pallas_call(
        kern,
        grid=(b, nh),
        in_specs=[pl.BlockSpec(blk, lambda bi, h: (head0 + h, bi, 0)),
                  pl.BlockSpec(blk, lambda bi, h: (head0 + nh + h, bi, 0)),
                  pl.BlockSpec(blk, lambda bi, h: (head0 + 2 * nh + h, bi, 0)),
                  pl.BlockSpec((None, WIN_ROWS, GRID_W, WIN_ROWS * GRID_W), lambda bi, h: (h, 0, 0, 0))],
        out_specs=pl.BlockSpec((t, HEAD_DIM), lambda bi, h: (bi, h)),
        out_shape=jax.ShapeDtypeStruct((n, nh * HEAD_DIM), F32),
        compiler_params=_params(("parallel", "arbitrary"), 48),
        name="nbr_attn",
    )(qkvh, qkvh, qkvh, bias_tab)


def _out_norms_kernel(oa_ref, ob_ref, ga_ref, gb_ref, out_ref):
    wa = oa_ref.shape[1]
    out_ref[:, :wa] = _rms(oa_ref[...], ga_ref[...]).astype(out_ref.dtype)
    out_ref[:, wa:] = _rms(ob_ref[...], gb_ref[...]).astype(out_ref.dtype)


def _out_norms(oa, ob, ga, gb):
    n, wa = oa.shape
    wb = ob.shape[1]
    bm = _pick(n, (512, 256, 128, 8))
    return pl.pallas_call(
        _out_norms_kernel,
        grid=(n // bm,),
        in_specs=[pl.BlockSpec((bm, wa), lambda i: (i, 0)),
                  pl.BlockSpec((bm, wb), lambda i: (i, 0)),
                  pl.BlockSpec((1, wa), lambda i: (0, 0)),
                  pl.BlockSpec((1, wb), lambda i: (0, 0))],
        out_specs=pl.BlockSpec((bm, wa + wb), lambda i: (i, 0)),
        out_shape=jax.ShapeDtypeStruct((n, wa + wb), BF16),
        compiler_params=_params(("parallel",), 40),
        name="out_norms",
    )(oa, ob, ga.reshape(1, wa), gb.reshape(1, wb))


def _layer(x, b, t, wts, late, final_gain):
    (ffn1_norm, ffn1, mix_norm, bias_tab, out_norm_a, out_norm_b, w_out, ffn2_norm) = wts
    wa = out_norm_a.shape[0]
    wb = out_norm_b.shape[0]
    nha, nhb = wa // HEAD_DIM, wb // HEAD_DIM
    rope = _rope_tables(t)

    if 'bf16' in late:
        h1 = _ffn(x, ffn1_norm, *ffn1)
    else:
        stacks, layer = late['f32']
        n_steps = _ffn_steps(x.shape[0], ffn1[0].shape[1])[2]
        if all(_cast_block(*w.shape[1:], n_steps) for w in stacks):
            h1, late['bf16'] = _ffn(x, ffn1_norm, *ffn1, cast_next=[(w, layer) for w in stacks])
        else:
            h1, late['bf16'] = _ffn(x, ffn1_norm, *ffn1), tuple(_cast_bf16(w, layer) for w in stacks)
    *ffn2, w_in = late['bf16']
    u = _rmsnorm(h1, mix_norm, BF16)

    qkv_a = _qkv_proj(u, w_in, 0, 3 * wa, rope, t, 2 * wa, F32)
    qkv_b = _qkv_proj(u, w_in, 3 * wa, 3 * wb, rope, t, 0, BF16)
    oa = _dilated_attention(qkv_a, b, t, nha)
    ob = _nbr_attention(qkv_b, bias_tab, b, t, 0, nhb)
    merged = _out_norms(oa, ob, out_norm_a, out_norm_b)
    h2 = _out_proj(merged, w_out, h1)

    return _ffn(h2, ffn2_norm, *ffn2, g_out=final_gain)


def kernel(x_prompt, x_sample, ffn1_norm, ffn1_w_gate, ffn1_w_up, ffn1_w_down, mix_norm, w_in, nbr_rel_bias, out_norm_a, out_norm_b, w_out, ffn2_norm, ffn2_w_gate, ffn2_w_up, ffn2_w_down, final_norm):
    depth = ffn1_norm.shape[0]
    layers, late = [], []
    for i in range(depth):
        layers.append((ffn1_norm[i],
                       tuple(_cast_bf16(w, i) for w in (ffn1_w_gate, ffn1_w_up, ffn1_w_down)),
                       mix_norm[i], _nbr_bias_tables(nbr_rel_bias[i]),
                       out_norm_a[i], out_norm_b[i], _cast_bf16(w_out, i), ffn2_norm[i]))
        late.append({'f32': ((ffn2_w_gate, ffn2_w_up, ffn2_w_down, w_in), i)})

    def run(x3):
        b, t, d = x3.shape
        h = x3.reshape(b * t, d)
        for i in range(depth):
            h = _layer(h, b, t, layers[i], late[i], final_norm if i == depth - 1 else None)
        return h.reshape(b, t, d)

    return run(x_prompt), run(x_sample)
```
